```python
import jax, jax.numpy as jnp
from jax import lax
import numpy as np

D_MODEL = 1024
BATCH = 16
SEQ = 4096
DEPTH = 2

GDN_HEADS = 4
GDN_DK = 128
GDN_DV = 128
GDN_CONV = 4
GDN_CHUNK = 64
HG_HEADS = 4
HG_DF = 128
HG_DI = 128
HG_CHUNK = 16
N_BRANCH = 2
NORM_EPS = 1e-6

GDN_QK = GDN_HEADS * GDN_DK
GDN_V = GDN_HEADS * GDN_DV
HG_F = HG_HEADS * HG_DF
HG_I = HG_HEADS * HG_DI
SPLIT_SIZES = [2 * GDN_QK + GDN_V,
               GDN_HEADS,
               GDN_HEADS,
               GDN_V,
               HG_F,
               HG_F,
               HG_I,
               HG_I,
               N_BRANCH * D_MODEL]
IN_COLS = sum(SPLIT_SIZES)
SPLIT_IDX = [int(v) for v in np.cumsum(SPLIT_SIZES)[:-1]]

kernel_name = "hybrid_gdn_hgrn2_gated_merge"


def _rmsnorm(x, w):
    xf = x.astype(jnp.float32)
    y = xf * lax.rsqrt(jnp.mean(xf * xf, axis=-1, keepdims=True) + NORM_EPS)
    return (y * w.astype(jnp.float32)).astype(x.dtype)


def _l2norm(t):
    t = t.astype(jnp.float32)
    return t * lax.rsqrt(jnp.sum(t * t, axis=-1, keepdims=True) + NORM_EPS)


def _causal_depthwise_conv(x, w):
    k = w.shape[0]
    return lax.conv_general_dilated(
        x, w[:, None, :].astype(x.dtype), window_strides=(1,), padding=[(k - 1, 0)],
        dimension_numbers=("NWC", "WIO", "NWC"), feature_group_count=x.shape[-1])


def _masked_exp(diff, mask):
    return jnp.where(mask, jnp.exp(jnp.where(mask, diff, 0.0)), 0.0)


def _to_chunks(t, c):
    b, s, h = t.shape[:3]
    t = t.reshape((b, s // c, c, h) + t.shape[3:])
    return t.transpose((1, 0, 3, 2) + tuple(range(4, t.ndim)))


def _from_chunks(t):
    n, b, h, c = t.shape[:4]
    t = t.transpose((1, 0, 3, 2) + tuple(range(4, t.ndim)))
    return t.reshape((b, n * c, h) + t.shape[4:])


def _gated_delta_rule(q, k, v, g, beta):
    f32 = jnp.float32
    c = GDN_CHUNK
    dk, dv = q.shape[-1], v.shape[-1]
    q, k, v, g, beta = (_to_chunks(t.astype(f32), c) for t in (q, k, v, g, beta))
    G = jnp.cumsum(g, axis=-1)
    causal = jnp.tril(jnp.ones((c, c), bool))
    strict = jnp.tril(jnp.ones((c, c), bool), -1)
    L = _masked_exp(G[..., :, None] - G[..., None, :], causal)
    kb = k * beta[..., None]
    A = jnp.where(strict, jnp.einsum("nbhid,nbhjd->nbhij", kb, k) * L, 0.0)
    M = A + jnp.eye(c, dtype=f32)
    rhs = jnp.concatenate([v * beta[..., None], kb * jnp.exp(G)[..., None]], axis=-1)
    sol = lax.linalg.triangular_solve(M, rhs, left_side=True, lower=True, unit_diagonal=True)
    u, w = sol[..., :dv], sol[..., dv:]
    qs = q * (dk ** -0.5)
    a_qk = jnp.einsum("nbhid,nbhjd->nbhij", qs, k) * L
    qg = qs * jnp.exp(G)[..., None]
    kdec = k * jnp.exp(G[..., -1:] - G)[..., None]
    gl = jnp.exp(G[..., -1])

    def step(S, inp):
        u_c, w_c, qg_c, a_c, kd_c, gl_c = inp
        v_new = u_c - jnp.einsum("bhck,bhkv->bhcv", w_c, S)
        o = jnp.einsum("bhck,bhkv->bhcv", qg_c, S) + jnp.einsum("bhij,bhjv->bhiv", a_c, v_new)
        S = gl_c[..., None, None] * S + jnp.einsum("bhck,bhcv->bhkv", kd_c, v_new)
        return S, o

    S0 = jnp.zeros(q.shape[1:3] + (dk, dv), f32)
    _, o = lax.scan(step, S0, (u, w, qg, a_qk, kdec, gl))
    return _from_chunks(o)


def _hgrn2_chunked(q, k, v, log_f):
    f32 = jnp.float32
    c = HG_CHUNK
    q, k, v, log_f = (_to_chunks(t.astype(f32), c) for t in (q, k, v, log_f))
    G = jnp.cumsum(log_f, axis=-2)
    qg = q * jnp.exp(G)
    kdec = k * jnp.exp(G[..., -1:, :] - G)
    gl = jnp.exp(G[..., -1, :])
    causal = jnp.tril(jnp.ones((c, c), bool))[:, :, None]

    def step(S, inp):
        q_c, k_c, G_c, v_c, qg_c, kd_c, gl_c = inp
        D = _masked_exp(G_c[..., :, None, :] - G_c[..., None, :, :], causal)
        a = jnp.einsum("bhif,bhjf,bhijf->bhij", q_c, k_c, D)
        o = jnp.einsum("bhij,bhjv->bhiv", a, v_c) + jnp.einsum("bhcf,bhfv->bhcv", qg_c, S)
        S = gl_c[..., :, None] * S + jnp.einsum("bhcf,bhcv->bhfv", kd_c, v_c)
        return S, o

    S0 = jnp.zeros(q.shape[1:3] + (q.shape[-1], v.shape[-1]), f32)
    _, o = lax.scan(step, S0, (q, k, G, v, qg, kdec, gl))
    return _from_chunks(o)


def setup_inputs(seed: int = 0) -> dict:
    key = jax.random.key(seed)
    ks = jax.random.split(key, 12)
    f32 = jnp.float32
    x = jax.random.normal(ks[0], (BATCH, SEQ, D_MODEL), f32)
    norm_w = 1.0 + 0.02 * jax.random.normal(ks[1], (DEPTH, D_MODEL), f32)
    w_in = jax.random.normal(ks[2], (DEPTH, D_MODEL, IN_COLS), f32) * D_MODEL ** -0.5
    conv_w = jax.random.normal(ks[3], (DEPTH, GDN_CONV, 2 * GDN_QK + GDN_V), f32) * GDN_CONV ** -0.5
    a_log = jnp.log(jax.random.uniform(ks[4], (DEPTH, GDN_HEADS), f32, 1.0, 16.0))
    dt = jnp.exp(jax.random.uniform(ks[5], (DEPTH, GDN_HEADS), f32, np.log(1e-3), np.log(1e-1)))
    dt_bias = dt + jnp.log(-jnp.expm1(-dt))
    gdn_norm_w = 1.0 + 0.02 * jax.random.normal(ks[6], (DEPTH, GDN_DV), f32)
    hg_lb_logits = 0.1 * jax.random.normal(ks[7], (DEPTH, HG_F), f32)
    hg_norm_w = 1.0 + 0.02 * jax.random.normal(ks[8], (DEPTH, HG_DI), f32)
    w_branch = jax.random.normal(ks[9], (DEPTH, N_BRANCH, GDN_V, D_MODEL), f32) * GDN_V ** -0.5
    w_out = jax.random.normal(ks[10], (DEPTH, D_MODEL, D_MODEL), f32) * D_MODEL ** -0.5
    final_norm_w = 1.0 + 0.02 * jax.random.normal(ks[11], (D_MODEL,), f32)
    return {"x": x, "norm_w": norm_w, "w_in": w_in, "conv_w": conv_w, "a_log": a_log,
            "dt_bias": dt_bias, "gdn_norm_w": gdn_norm_w, "hg_lb_logits": hg_lb_logits,
            "hg_norm_w": hg_norm_w, "w_branch": w_branch, "w_out": w_out,
            "final_norm_w": final_norm_w}


def reference(x, norm_w, w_in, conv_w, a_log, dt_bias, gdn_norm_w, hg_lb_logits,
              hg_norm_w, w_branch, w_out, final_norm_w):
    f32 = jnp.float32
    b, s, _ = x.shape
    lb_p = jax.nn.softmax(hg_lb_logits.astype(f32), axis=0)
    lower_bounds = jnp.cumsum(lb_p, axis=0) - lb_p[0]
    h = x
    for l in range(DEPTH):
        xn = _rmsnorm(h, norm_w[l])
        proj = jnp.einsum("bsd,dc->bsc", xn, w_in[l])
        qkv, a_in, b_in, z_a, hq, hf, hi, z_b, gate = jnp.split(proj, SPLIT_IDX, axis=-1)

        qkv = jax.nn.silu(_causal_depthwise_conv(qkv, conv_w[l]))
        q, k, v = jnp.split(qkv, [GDN_QK, 2 * GDN_QK], axis=-1)
        q = _l2norm(q.reshape(b, s, GDN_HEADS, GDN_DK))
        k = _l2norm(k.reshape(b, s, GDN_HEADS, GDN_DK))
        v = v.reshape(b, s, GDN_HEADS, GDN_DV)
        g = -jnp.exp(a_log[l].astype(f32)) * jax.nn.softplus(a_in.astype(f32) + dt_bias[l].astype(f32))
        beta = jax.nn.sigmoid(b_in.astype(f32))
        o_a = _gated_delta_rule(q, k, v, g, beta)
        y_a = _rmsnorm(o_a, gdn_norm_w[l]) * jax.nn.silu(z_a.astype(f32).reshape(b, s, GDN_HEADS, GDN_DV))
        y_a = y_a.reshape(b, s, GDN_V).astype(x.dtype)

        lb = lower_bounds[l].reshape(HG_HEADS, HG_DF)
        f_logit = hf.astype(f32).reshape(b, s, HG_HEADS, HG_DF)
        f_gate = lb + (1.0 - lb) * jax.nn.sigmoid(f_logit)
        log_f = jnp.log(f_gate)
        k_h = (1.0 - lb) * jax.nn.sigmoid(-f_logit)
        o_b = _hgrn2_chunked(hq.reshape(b, s, HG_HEADS, HG_DF), k_h,
                             hi.reshape(b, s, HG_HEADS, HG_DI), log_f)
        y_b = _rmsnorm(o_b, hg_norm_w[l]) * jax.nn.silu(z_b.astype(f32).reshape(b, s, HG_HEADS, HG_DI))
        y_b = y_b.reshape(b, s, HG_I).astype(x.dtype)

        gate_a, gate_b = jnp.split(gate, 2, axis=-1)
        merged = (jax.nn.sigmoid(gate_a) * jnp.einsum("bsw,wd->bsd", y_a, w_branch[l, 0])
                  + jax.nn.sigmoid(gate_b) * jnp.einsum("bsw,wd->bsd", y_b, w_branch[l, 1]))
        h = h + jnp.einsum("bsd,de->bse", merged, w_out[l])
    return _rmsnorm(h, final_norm_w)
```

```python
import functools

import jax
import jax.numpy as jnp
from jax import lax
from jax.experimental import pallas as pl
from jax.experimental.pallas import tpu as pltpu

F32 = jnp.float32
BF16 = jnp.bfloat16

D_MODEL = 1024
HEADS = 4
HD = 128
BR = HEADS * HD
QKV = 3 * BR
GDN_CONV = 4
NORM_EPS = 1e-6
LANES = 128
SUBLANES = 8

TM = 256
GC = 128
GC_BITS = 7
HC = 16
HB = 128
VMEM_LIMIT = 56 * 1024 * 1024


def _sigmoid(x):
    return 1.0 / (1.0 + jnp.exp(-x))


def _silu(x):
    return x * _sigmoid(x)


def _dot(a, b):
    return jnp.dot(a.astype(BF16), b.astype(BF16), preferred_element_type=F32)


def _dot_nt(a, b):
    return lax.dot_general(a.astype(BF16), b.astype(BF16), (((1,), (1,)), ((), ())),
                           preferred_element_type=F32)


def _dot_01(m01, x):
    x1 = x.astype(BF16)
    r1 = x - x1.astype(F32)
    x2 = r1.astype(BF16)
    x3 = (r1 - x2.astype(F32)).astype(BF16)
    return (jnp.dot(m01, x1, preferred_element_type=F32)
            + jnp.dot(m01, x2, preferred_element_type=F32)
            + jnp.dot(m01, x3, preferred_element_type=F32))


def _layer_kernel(layer, depth, is_last,
                  h_ref, nw_ref, wqkv_ref, wab_ref, wza_ref, whq_ref, whf_ref, whi_ref, wzb_ref,
                  wg_ref, cw_ref, alog_ref, dtb_ref, gnw_ref, lbl_ref, hnw_ref, wa_ref, wb_ref,
                  wo_ref, fnw_ref,
                  out_ref,
                  xn_s, qkv_s, q_s, k_s, v_s, ab_s, za_s, zb_s, hq_s, hf_s, hk_s, hi_s, ya_s, yb_s,
                  sg_s, sh_s):
    j = pl.program_id(1)

    @pl.when(j == 0)
    def _():
        qkv_s[0:SUBLANES, :] = jnp.zeros((SUBLANES, QKV), F32)
        sg_s[...] = jnp.zeros_like(sg_s)
        sh_s[...] = jnp.zeros_like(sh_s)

    h = h_ref[0]
    ms = jnp.mean(h * h, axis=-1, keepdims=True)
    xn_s[...] = (h * lax.rsqrt(ms + NORM_EPS) * nw_ref[...]).astype(BF16)

    qkv_s[SUBLANES:SUBLANES + TM, :] = jnp.dot(xn_s[...], wqkv_ref[...], preferred_element_type=F32)
    ab_s[...] = jnp.dot(xn_s[...], wab_ref[...], preferred_element_type=F32)
    za_s[...] = _silu(jnp.dot(xn_s[...], wza_ref[...], preferred_element_type=F32))
    zb_s[...] = _silu(jnp.dot(xn_s[...], wzb_ref[...], preferred_element_type=F32))
    hq_s[...] = jnp.dot(xn_s[...], whq_ref[...], preferred_element_type=F32)
    hi_s[...] = jnp.dot(xn_s[...], whi_ref[...], preferred_element_type=F32)

    lbl = lbl_ref[...]
    lbe = jnp.exp(lbl - jnp.max(lbl, axis=0, keepdims=True))
    lbp = lbe / jnp.sum(lbe, axis=0, keepdims=True)
    lb = jnp.zeros((1, BR), F32)
    for i in range(1, layer + 1):
        lb = lb + lbp[i:i + 1, :]
    f_logit = jnp.dot(xn_s[...], whf_ref[...], preferred_element_type=F32)
    e = jnp.exp(-jnp.abs(f_logit))
    r = 1.0 / (1.0 + e)
    pos = f_logit >= 0
    sig_p = jnp.where(pos, r, e * r)
    sig_n = jnp.where(pos, e * r, r)
    hf_s[...] = lb + (1.0 - lb) * sig_p
    hk_s[...] = (1.0 - lb) * sig_n

    for cb in range(QKV // LANES):
        cs = slice(cb * LANES, (cb + 1) * LANES)
        acc = cw_ref[0:1, cs] * qkv_s[SUBLANES - 3:SUBLANES - 3 + TM, cs]
        for t in range(1, GDN_CONV):
            acc = acc + cw_ref[t:t + 1, cs] * qkv_s[SUBLANES - 3 + t:SUBLANES - 3 + t + TM, cs]
        c = _silu(acc)
        if cb < 2 * HEADS:
            c = c * lax.rsqrt(jnp.sum(c * c, axis=-1, keepdims=True) + NORM_EPS)
            if cb < HEADS:
                q_s[:, cs] = c
            else:
                k_s[:, (cb - HEADS) * LANES:(cb - HEADS + 1) * LANES] = c
        else:
            v_s[:, (cb - 2 * HEADS) * LANES:(cb - 2 * HEADS + 1) * LANES] = c
    qkv_s[0:SUBLANES, :] = qkv_s[TM:TM + SUBLANES, :]

    row = lax.broadcasted_iota(jnp.int32, (GC, GC), 0)
    col = lax.broadcasted_iota(jnp.int32, (GC, GC), 1)
    causal = row >= col
    strict = row > col
    tril01 = causal.astype(BF16)
    eye = (row == col).astype(F32)
    rxc = row ^ col
    lev = jnp.full((GC, GC), -1, jnp.int32)
    for bit in range(GC_BITS):
        lev = lev + (rxc >= (1 << bit)).astype(jnp.int32)
    lev = jnp.where(strict, lev, -1)
    qscale = HD ** -0.5

    def gdn_chunk(c, carry):
        r0 = pl.multiple_of(c * GC, GC)
        rows = pl.ds(r0, GC)
        ab = ab_s[rows, :]
        x = ab + dtb_ref[...]
        softplus = jnp.maximum(x, 0.0) + jnp.log(1.0 + jnp.exp(-jnp.abs(x)))
        g = -jnp.exp(alog_ref[...]) * softplus
        beta_all = _sigmoid(ab)
        gcum = _dot_01(tril01, g)
        gcum_t = gcum.T
        eg_all = jnp.exp(gcum)
        glast = gcum[GC - 1:GC, :]
        egd_all = jnp.exp(glast - gcum)
        for hd in range(HEADS):
            hs = slice(hd * HD, (hd + 1) * HD)
            q = q_s[rows, hs]
            k = k_s[rows, hs]
            v = v_s[rows, hs]
            beta = beta_all[:, HEADS + hd:HEADS + hd + 1]
            eg = eg_all[:, hd:hd + 1]
            egd = egd_all[:, hd:hd + 1]
            gl = eg_all[GC - 1:GC, hd:hd + 1]
            kb = k * beta
            qs = q * qscale
            kq = _dot_nt(jnp.concatenate([kb.astype(BF16), qs.astype(BF16)], axis=0), k)
            diff = gcum[:, hd:hd + 1] - gcum_t[hd:hd + 1, :]
            lmat = jnp.where(causal, jnp.exp(jnp.minimum(diff, 0.0)), 0.0)
            a = jnp.where(strict, kq[:GC] * lmat, 0.0)
            a_qk = kq[GC:] * lmat
            t_inv = eye - jnp.where(lev == 0, a, 0.0)
            for l in range(1, GC_BITS):
                e_l = jnp.where(lev == l, a, 0.0)
                t_inv = t_inv - _dot(t_inv, _dot(e_l, t_inv))
            xs = _dot(t_inv, jnp.concatenate([v * beta, kb * eg], axis=1))
            u = xs[:, :HD]
            w = xs[:, HD:]
            s = sg_s[hd]
            wq = _dot(jnp.concatenate([w.astype(BF16), (qs * eg).astype(BF16)], axis=0), s)
            v_new = u - wq[:GC]
            kd_t = (k * egd).T
            av = _dot(jnp.concatenate([a_qk.astype(BF16), kd_t.astype(BF16)], axis=0), v_new)
            o = wq[GC:] + av[:GC]
            sg_s[hd] = gl * s + av[GC:]
            on = o * lax.rsqrt(jnp.mean(o * o, axis=-1, keepdims=True) + NORM_EPS) * gnw_ref[...]
            ya_s[rows, hs] = (on * za_s[rows, hs]).astype(BF16)
        return carry

    lax.fori_loop(0, TM // GC, gdn_chunk, 0)

    rowb = lax.broadcasted_iota(jnp.int32, (HB, HB), 0)
    colb = lax.broadcasted_iota(jnp.int32, (HB, HB), 1)
    hc_bits = HC.bit_length() - 1
    same_chunk = (rowb >> hc_bits) == (colb >> hc_bits)
    dmat = jnp.where(same_chunk, rowb - colb, -1)
    posc = rowb & (HC - 1)
    colv = colb >> hc_bits

    def hg_block(bi, carry):
        r0 = pl.multiple_of(bi * HB, HB)
        rows = pl.ds(r0, HB)
        for hd in range(HEADS):
            hs = slice(hd * HD, (hd + 1) * HD)
            q = hq_s[rows, hs]
            f = hf_s[rows, hs]
            kk = hk_s[rows, hs]
            v = hi_s[rows, hs]
            pre = f
            suf = f
            for sft in (1, 2, 4, 8):
                pre = pre * jnp.where(posc >= sft, pltpu.roll(pre, sft, axis=0), 1.0)
                suf = suf * jnp.where(posc < HC - sft, pltpu.roll(suf, HB - sft, axis=0), 1.0)
            suf = jnp.where(posc < HC - 1, pltpu.roll(suf, HB - 1, axis=0), 1.0)
            qg = q * pre
            kd = kk * suf
            rr = kk
            amat = jnp.where(dmat == 0, jnp.sum(q * rr, axis=-1, keepdims=True), 0.0)
            for d in range(1, HC):
                rr = pltpu.roll(rr, 1, axis=0) * f
                amat = jnp.where(dmat == d, jnp.sum(q * rr, axis=-1, keepdims=True), amat)
            o = _dot(amat, v)
            vt = v.T
            st = sh_s[hd]
            o_parts = []
            for ci in range(HB // HC):
                qg_c = qg[ci * HC:(ci + 1) * HC, :]
                o_parts.append(_dot_nt(qg_c, st))
                upd = _dot(jnp.where(colv == ci, vt, 0.0), kd)
                gl = pre[(ci + 1) * HC - 1:(ci + 1) * HC, :]
                st = st * gl + upd
            sh_s[hd] = st
            o = o + jnp.concatenate(o_parts, axis=0)
            on = o * lax.rsqrt(jnp.mean(o * o, axis=-1, keepdims=True) + NORM_EPS) * hnw_ref[...]
            yb_s[rows, hs] = (on * zb_s[rows, hs]).astype(BF16)
        return carry

    lax.fori_loop(0, TM // HB, hg_block, 0)

    gate_a = _sigmoid(jnp.dot(xn_s[...], wg_ref[:, :D_MODEL], preferred_element_type=F32))
    merged = gate_a * jnp.dot(ya_s[...], wa_ref[...], preferred_element_type=F32)
    gate_b = _sigmoid(jnp.dot(xn_s[...], wg_ref[:, D_MODEL:], preferred_element_type=F32))
    merged = merged + gate_b * jnp.dot(yb_s[...], wb_ref[...], preferred_element_type=F32)
    out = h_ref[0] + jnp.dot(merged.astype(BF16), wo_ref[...], preferred_element_type=F32)
    if is_last:
        out = out * lax.rsqrt(jnp.mean(out * out, axis=-1, keepdims=True) + NORM_EPS) * fnw_ref[...]
    out_ref[0] = out


def _full(shape):
    nd = len(shape)
    return pl.BlockSpec(shape, lambda b, j: (0,) * nd, pipeline_mode=pl.Buffered(1))


def _layer(h, layer, depth, is_last, p):
    bsz, seq, _ = h.shape
    assert seq % TM == 0 and TM % GC == 0 and TM % HB == 0 and HB % HC == 0
    weights = [p["nw"], p["wqkv"], p["wab"], p["wza"], p["whq"], p["whf"], p["whi"], p["wzb"],
               p["wg"], p["cw"], p["alog"], p["dtb"], p["gnw"], p["lbl"], p["hnw"], p["wa"], p["wb"],
               p["wo"], p["fnw"]]
    tile = pl.BlockSpec((1, TM, D_MODEL), lambda b, j: (b, j, 0))
    scratch = [
        pltpu.VMEM((TM, D_MODEL), BF16),
        pltpu.VMEM((TM + SUBLANES, QKV), F32),
        pltpu.VMEM((TM, BR), F32),
        pltpu.VMEM((TM, BR), F32),
        pltpu.VMEM((TM, BR), F32),
        pltpu.VMEM((TM, LANES), F32),
        pltpu.VMEM((TM, BR), F32),
        pltpu.VMEM((TM, BR), F32),
        pltpu.VMEM((TM, BR), F32),
        pltpu.VMEM((TM, BR), F32),
        pltpu.VMEM((TM, BR), F32),
        pltpu.VMEM((TM, BR), F32),
        pltpu.VMEM((TM, BR), BF16),
        pltpu.VMEM((TM, BR), BF16),
        pltpu.VMEM((HEADS, HD, HD), F32),
        pltpu.VMEM((HEADS, HD, HD), F32),
    ]
    return pl.pallas_call(
        functools.partial(_layer_kernel, layer, depth, is_last),
        grid=(bsz, seq // TM),
        in_specs=[tile] + [_full(w.shape) for w in weights],
        out_specs=tile,
        out_shape=jax.ShapeDtypeStruct(h.shape, F32),
        scratch_shapes=scratch,
        compiler_params=pltpu.CompilerParams(
            dimension_semantics=("arbitrary", "arbitrary"), vmem_limit_bytes=VMEM_LIMIT),
        name=f"hybrid_layer{layer}",
    )(h, *weights)


def kernel(x, norm_w, w_in, conv_w, a_log, dt_bias, gdn_norm_w, hg_lb_logits, hg_norm_w, w_branch,
           w_out, final_norm_w):
    depth = w_in.shape[0]
    o_ab = QKV
    o_za = o_ab + 2 * HEADS
    o_hq = o_za + BR
    o_hf = o_hq + BR
    o_hi = o_hf + BR
    o_zb = o_hi + BR
    o_g = o_zb + BR

    def pad_lanes(t):
        return jnp.pad(t, ((0, 0), (0, LANES - t.shape[1])))

    h = x
    for l in range(depth):
        wl = w_in[l].astype(BF16)
        p = {
            "nw": norm_w[l][None, :],
            "wqkv": wl[:, :o_ab],
            "wab": pad_lanes(wl[:, o_ab:o_za]),
            "wza": wl[:, o_za:o_hq],
            "whq": wl[:, o_hq:o_hf],
            "whf": wl[:, o_hf:o_hi],
            "whi": wl[:, o_hi:o_zb],
            "wzb": wl[:, o_zb:o_g],
            "wg": wl[:, o_g:],
            "cw": conv_w[l],
            "alog": pad_lanes(a_log[l][None, :]),
            "dtb": pad_lanes(dt_bias[l][None, :]),
            "gnw": gdn_norm_w[l][None, :],
            "lbl": hg_lb_logits,
            "hnw": hg_norm_w[l][None, :],
            "wa": w_branch[l, 0].astype(BF16),
            "wb": w_branch[l, 1].astype(BF16),
            "wo": w_out[l].astype(BF16),
            "fnw": final_norm_w[None, :],
        }
        h = _layer(h, l, depth, l == depth - 1, p)
    return h
```

```python
import functools

import jax
import jax.numpy as jnp
from jax import lax
from jax.experimental import pallas as pl
from jax.experimental.pallas import tpu as pltpu

F32 = jnp.float32
BF16 = jnp.bfloat16

D_MODEL = 1024
HEADS = 4
HD = 128
BR = HEADS * HD
QKV = 3 * BR
GDN_CONV = 4
NORM_EPS = 1e-6
LANES = 128
SUBLANES = 8

TM = 256
GC = 128
GC_BITS = 7
HC = 16
HB = 128
VMEM_LIMIT = 56 * 1024 * 1024


def _sigmoid(x):
    return 1.0 / (1.0 + jnp.exp(-x))


def _silu(x):
    return x * _sigmoid(x)


def _dot(a, b):
    return jnp.dot(a.astype(BF16), b.astype(BF16), preferred_element_type=F32)


def _dot_nt(a, b):
    return lax.dot_general(a.astype(BF16), b.astype(BF16), (((1,), (1,)), ((), ())),
                           preferred_element_type=F32)


def _dot_01(m01, x):
    x1 = x.astype(BF16)
    r1 = x - x1.astype(F32)
    x2 = r1.astype(BF16)
    x3 = (r1 - x2.astype(F32)).astype(BF16)
    return (jnp.dot(m01, x1, preferred_element_type=F32)
            + jnp.dot(m01, x2, preferred_element_type=F32)
            + jnp.dot(m01, x3, preferred_element_type=F32))


def _layer_kernel(layer, depth, is_last,
                  h_ref, nw_ref, wqkv_ref, wab_ref, wza_ref, whq_ref, whf_ref, whi_ref, wzb_ref,
                  wg_ref, cw_ref, alog_ref, dtb_ref, gnw_ref, lbl_ref, hnw_ref, wa_ref, wb_ref,
                  wo_ref, fnw_ref,
                  out_ref,
                  xn_s, qkv_s, q_s, k_s, v_s, ab_s, za_s, zb_s, hq_s, hf_s, hk_s, hi_s, ya_s, yb_s,
                  aqk_s, sg_s, sh_s):
    j = pl.program_id(1)

    @pl.when(j == 0)
    def _():
        qkv_s[0:SUBLANES, :] = jnp.zeros((SUBLANES, QKV), F32)
        sg_s[...] = jnp.zeros_like(sg_s)
        sh_s[...] = jnp.zeros_like(sh_s)

    h = h_ref[0]
    ms = jnp.mean(h * h, axis=-1, keepdims=True)
    xn_s[...] = (h * lax.rsqrt(ms + NORM_EPS) * nw_ref[...]).astype(BF16)

    qkv_s[SUBLANES:SUBLANES + TM, :] = jnp.dot(xn_s[...], wqkv_ref[...], preferred_element_type=F32)
    ab_s[...] = jnp.dot(xn_s[...], wab_ref[...], preferred_element_type=F32)
    za_s[...] = _silu(jnp.dot(xn_s[...], wza_ref[...], preferred_element_type=F32))
    zb_s[...] = _silu(jnp.dot(xn_s[...], wzb_ref[...], preferred_element_type=F32))
    hq_s[...] = jnp.dot(xn_s[...], whq_ref[...], preferred_element_type=F32)
    hi_s[...] = jnp.dot(xn_s[...], whi_ref[...], preferred_element_type=F32)

    lbl = lbl_ref[...]
    lbe = jnp.exp(lbl - jnp.max(lbl, axis=0, keepdims=True))
    lbp = lbe / jnp.sum(lbe, axis=0, keepdims=True)
    lb = jnp.zeros((1, BR), F32)
    for i in range(1, layer + 1):
        lb = lb + lbp[i:i + 1, :]
    f_logit = jnp.dot(xn_s[...], whf_ref[...], preferred_element_type=F32)
    e = jnp.exp(-jnp.abs(f_logit))
    r = 1.0 / (1.0 + e)
    pos = f_logit >= 0
    sig_p = jnp.where(pos, r, e * r)
    sig_n = jnp.where(pos, e * r, r)
    hf_s[...] = lb + (1.0 - lb) * sig_p
    hk_s[...] = (1.0 - lb) * sig_n

    for cb in range(QKV // LANES):
        cs = slice(cb * LANES, (cb + 1) * LANES)
        acc = cw_ref[0:1, cs] * qkv_s[SUBLANES - 3:SUBLANES - 3 + TM, cs]
        for t in range(1, GDN_CONV):
            acc = acc + cw_ref[t:t + 1, cs] * qkv_s[SUBLANES - 3 + t:SUBLANES - 3 + t + TM, cs]
        c = _silu(acc)
        if cb < 2 * HEADS:
            c = c * lax.rsqrt(jnp.sum(c * c, axis=-1, keepdims=True) + NORM_EPS)
            if cb < HEADS:
                q_s[:, cs] = c
            else:
                k_s[:, (cb - HEADS) * LANES:(cb - HEADS + 1) * LANES] = c
        else:
            v_s[:, (cb - 2 * HEADS) * LANES:(cb - 2 * HEADS + 1) * LANES] = c
    qkv_s[0:SUBLANES, :] = qkv_s[TM:TM + SUBLANES, :]

    row = lax.broadcasted_iota(jnp.int32, (GC, GC), 0)
    col = lax.broadcasted_iota(jnp.int32, (GC, GC), 1)
    causal = row >= col
    strict = row > col
    tril01 = causal.astype(BF16)
    eye = (row == col).astype(F32)
    rxc = row ^ col
    lev = jnp.full((GC, GC), -1, jnp.int32)
    for bit in range(GC_BITS):
        lev = lev + (rxc >= (1 << bit)).astype(jnp.int32)
    lev = jnp.where(strict, lev, -1)
    qscale = HD ** -0.5

    n_chunks = TM // GC
    chunk_rows = [slice(c * GC, (c + 1) * GC) for c in range(n_chunks)]
    head_cols = [slice(hd * HD, (hd + 1) * HD) for hd in range(HEADS)]
    pairs = [(c, hd) for c in range(n_chunks) for hd in range(HEADS)]

    gcum, eg_all, egd_all, beta_all, gcum_t = [], [], [], [], []
    for rows in chunk_rows:
        ab = ab_s[rows, :]
        x = ab + dtb_ref[...]
        softplus = jnp.maximum(x, 0.0) + jnp.log(1.0 + jnp.exp(-jnp.abs(x)))
        g = -jnp.exp(alog_ref[...]) * softplus
        beta_all.append(_sigmoid(ab))
        gc = _dot_01(tril01, g)
        gcum.append(gc)
        gcum_t.append(gc.T)
        eg_all.append(jnp.exp(gc))
        egd_all.append(jnp.exp(gc[GC - 1:GC, :] - gc))

    a_mats, rhs = [], []
    for c, hd in pairs:
        rows, hs = chunk_rows[c], head_cols[hd]
        k = k_s[rows, hs]
        beta = beta_all[c][:, HEADS + hd:HEADS + hd + 1]
        kb = k * beta
        qs = q_s[rows, hs] * qscale
        kq = _dot_nt(jnp.concatenate([kb.astype(BF16), qs.astype(BF16)], axis=0), k)
        diff = gcum[c][:, hd:hd + 1] - gcum_t[c][hd:hd + 1, :]
        lmat = jnp.where(causal, jnp.exp(jnp.minimum(diff, 0.0)), 0.0)
        a_mats.append(jnp.where(strict, kq[:GC] * lmat, 0.0))
        aqk_s[c * HEADS + hd] = (kq[GC:] * lmat).astype(BF16)
        rhs.append(jnp.concatenate([v_s[rows, hs] * beta, kb * eg_all[c][:, hd:hd + 1]],
                                   axis=1).astype(BF16))

    t_inv = [eye - jnp.where(lev == 0, a, 0.0) for a in a_mats]
    for l in range(1, GC_BITS):
        t_bf = [t.astype(BF16) for t in t_inv]
        et = [jnp.dot(jnp.where(lev == l, a, 0.0).astype(BF16), t, preferred_element_type=F32)
              for a, t in zip(a_mats, t_bf)]
        t_inv = [t - jnp.dot(tb, e.astype(BF16), preferred_element_type=F32)
                 for t, tb, e in zip(t_inv, t_bf, et)]
    uw = [jnp.dot(t.astype(BF16), r, preferred_element_type=F32) for t, r in zip(t_inv, rhs)]

    for c in range(n_chunks):
        rows = chunk_rows[c]
        wq, kd_t = [], []
        for hd in range(HEADS):
            hs = head_cols[hd]
            qg = q_s[rows, hs] * (qscale * eg_all[c][:, hd:hd + 1])
            w = uw[c * HEADS + hd][:, HD:]
            wq.append(_dot(jnp.concatenate([w.astype(BF16), qg.astype(BF16)], axis=0), sg_s[hd]))
            kd_t.append((k_s[rows, hs] * egd_all[c][:, hd:hd + 1]).T.astype(BF16))
        for hd in range(HEADS):
            hs = head_cols[hd]
            v_new = uw[c * HEADS + hd][:, :HD] - wq[hd][:GC]
            av = _dot(jnp.concatenate([aqk_s[c * HEADS + hd], kd_t[hd]], axis=0), v_new)
            o = wq[hd][GC:] + av[:GC]
            sg_s[hd] = eg_all[c][GC - 1:GC, hd:hd + 1] * sg_s[hd] + av[GC:]
            on = o * lax.rsqrt(jnp.mean(o * o, axis=-1, keepdims=True) + NORM_EPS) * gnw_ref[...]
            ya_s[rows, hs] = (on * za_s[rows, hs]).astype(BF16)

    rowb = lax.broadcasted_iota(jnp.int32, (HB, HB), 0)
    colb = lax.broadcasted_iota(jnp.int32, (HB, HB), 1)
    hc_bits = HC.bit_length() - 1
    same_chunk = (rowb >> hc_bits) == (colb >> hc_bits)
    dmat = jnp.where(same_chunk, rowb - colb, -1)
    posc = rowb & (HC - 1)
    colv = colb >> hc_bits

    def hg_block(bi, carry):
        r0 = pl.multiple_of(bi * HB, HB)
        rows = pl.ds(r0, HB)
        for hd in range(HEADS):
            hs = slice(hd * HD, (hd + 1) * HD)
            q = hq_s[rows, hs]
            f = hf_s[rows, hs]
            kk = hk_s[rows, hs]
            v = hi_s[rows, hs]
            pre = f
            suf = f
            for sft in (1, 2, 4, 8):
                pre = pre * jnp.where(posc >= sft, pltpu.roll(pre, sft, axis=0), 1.0)
                suf = suf * jnp.where(posc < HC - sft, pltpu.roll(suf, HB - sft, axis=0), 1.0)
            suf = jnp.where(posc < HC - 1, pltpu.roll(suf, HB - 1, axis=0), 1.0)
            qg = q * pre
            kd = kk * suf
            rr = kk
            amat = jnp.where(dmat == 0, jnp.sum(q * rr, axis=-1, keepdims=True), 0.0)
            for d in range(1, HC):
                rr = pltpu.roll(rr, 1, axis=0) * f
                amat = jnp.where(dmat == d, jnp.sum(q * rr, axis=-1, keepdims=True), amat)
            o = _dot(amat, v)
            vt = v.T
            st = sh_s[hd]
            o_parts = []
            for ci in range(HB // HC):
                qg_c = qg[ci * HC:(ci + 1) * HC, :]
                o_parts.append(_dot_nt(qg_c, st))
                upd = _dot(jnp.where(colv == ci, vt, 0.0), kd)
                gl = pre[(ci + 1) * HC - 1:(ci + 1) * HC, :]
                st = st * gl + upd
            sh_s[hd] = st
            o = o + jnp.concatenate(o_parts, axis=0)
            on = o * lax.rsqrt(jnp.mean(o * o, axis=-1, keepdims=True) + NORM_EPS) * hnw_ref[...]
            yb_s[rows, hs] = (on * zb_s[rows, hs]).astype(BF16)
        return carry

    lax.fori_loop(0, TM // HB, hg_block, 0)

    gate_a = _sigmoid(jnp.dot(xn_s[...], wg_ref[:, :D_MODEL], preferred_element_type=F32))
    merged = gate_a * jnp.dot(ya_s[...], wa_ref[...], preferred_element_type=F32)
    gate_b = _sigmoid(jnp.dot(xn_s[...], wg_ref[:, D_MODEL:], preferred_element_type=F32))
    merged = merged + gate_b * jnp.dot(yb_s[...], wb_ref[...], preferred_element_type=F32)
    out = h_ref[0] + jnp.dot(merged.astype(BF16), wo_ref[...], preferred_element_type=F32)
    if is_last:
        out = out * lax.rsqrt(jnp.mean(out * out, axis=-1, keepdims=True) + NORM_EPS) * fnw_ref[...]
    out_ref[0] = out


def _full(shape):
    nd = len(shape)
    return pl.BlockSpec(shape, lambda b, j: (0,) * nd, pipeline_mode=pl.Buffered(1))


def _layer(h, layer, depth, is_last, p):
    bsz, seq, _ = h.shape
    assert seq % TM == 0 and TM % GC == 0 and TM % HB == 0 and HB % HC == 0
    weights = [p["nw"], p["wqkv"], p["wab"], p["wza"], p["whq"], p["whf"], p["whi"], p["wzb"],
               p["wg"], p["cw"], p["alog"], p["dtb"], p["gnw"], p["lbl"], p["hnw"], p["wa"], p["wb"],
               p["wo"], p["fnw"]]
    tile = pl.BlockSpec((1, TM, D_MODEL), lambda b, j: (b, j, 0))
    scratch = [
        pltpu.VMEM((TM, D_MODEL), BF16),
        pltpu.VMEM((TM + SUBLANES, QKV), F32),
        pltpu.VMEM((TM, BR), F32),
        pltpu.VMEM((TM, BR), F32),
        pltpu.VMEM((TM, BR), F32),
        pltpu.VMEM((TM, LANES), F32),
        pltpu.VMEM((TM, BR), F32),
        pltpu.VMEM((TM, BR), F32),
        pltpu.VMEM((TM, BR), F32),
        pltpu.VMEM((TM, BR), F32),
        pltpu.VMEM((TM, BR), F32),
        pltpu.VMEM((TM, BR), F32),
        pltpu.VMEM((TM, BR), BF16),
        pltpu.VMEM((TM, BR), BF16),
        pltpu.VMEM((TM // GC * HEADS, GC, GC), BF16),
        pltpu.VMEM((HEADS, HD, HD), F32),
        pltpu.VMEM((HEADS, HD, HD), F32),
    ]
    return pl.pallas_call(
        functools.partial(_layer_kernel, layer, depth, is_last),
        grid=(bsz, seq // TM),
        in_specs=[tile] + [_full(w.shape) for w in weights],
        out_specs=tile,
        out_shape=jax.ShapeDtypeStruct(h.shape, F32),
        scratch_shapes=scratch,
        compiler_params=pltpu.CompilerParams(
            dimension_semantics=("arbitrary", "arbitrary"), vmem_limit_bytes=VMEM_LIMIT),
        name=f"hybrid_layer{layer}",
    )(h, *weights)


def kernel(x, norm_w, w_in, conv_w, a_log, dt_bias, gdn_norm_w, hg_lb_logits, hg_norm_w, w_branch,
           w_out, final_norm_w):
    depth = w_in.shape[0]
    o_ab = QKV
    o_za = o_ab + 2 * HEADS
    o_hq = o_za + BR
    o_hf = o_hq + BR
    o_hi = o_hf + BR
    o_zb = o_hi + BR
    o_g = o_zb + BR

    def pad_lanes(t):
        return jnp.pad(t, ((0, 0), (0, LANES - t.shape[1])))

    h = x
    for l in range(depth):
        wl = w_in[l].astype(BF16)
        p = {
            "nw": norm_w[l][None, :],
            "wqkv": wl[:, :o_ab],
            "wab": pad_lanes(wl[:, o_ab:o_za]),
            "wza": wl[:, o_za:o_hq],
            "whq": wl[:, o_hq:o_hf],
            "whf": wl[:, o_hf:o_hi],
            "whi": wl[:, o_hi:o_zb],
            "wzb": wl[:, o_zb:o_g],
            "wg": wl[:, o_g:],
            "cw": conv_w[l],
            "alog": pad_lanes(a_log[l][None, :]),
            "dtb": pad_lanes(dt_bias[l][None, :]),
            "gnw": gdn_norm_w[l][None, :],
            "lbl": hg_lb_logits,
            "hnw": hg_norm_w[l][None, :],
            "wa": w_branch[l, 0].astype(BF16),
            "wb": w_branch[l, 1].astype(BF16),
            "wo": w_out[l].astype(BF16),
            "fnw": final_norm_w[None, :],
        }
        h = _layer(h, l, depth, l == depth - 1, p)
    return h
```

```python
import functools

import jax
import jax.numpy as jnp
from jax import lax
from jax.experimental import pallas as pl
from jax.experimental.pallas import tpu as pltpu

F32 = jnp.float32
BF16 = jnp.bfloat16

D_MODEL = 1024
HEADS = 4
HD = 128
BR = HEADS * HD
QKV = 3 * BR
GDN_CONV = 4
NORM_EPS = 1e-6
LANES = 128
SUBLANES = 8

TM = 256
GC = 128
GC_BITS = 7
HC = 16
HB = 128
VMEM_LIMIT = 56 * 1024 * 1024


def _sigmoid(x):
    return 1.0 / (1.0 + jnp.exp(-x))


def _silu(x):
    return x * _sigmoid(x)


def _dot(a, b):
    return jnp.dot(a.astype(BF16), b.astype(BF16), preferred_element_type=F32)


def _dot_nt(a, b):
    return lax.dot_general(a.astype(BF16), b.astype(BF16), (((1,), (1,)), ((), ())),
                           preferred_element_type=F32)


def _dot_01(m01, x):
    x1 = x.astype(BF16)
    r1 = x - x1.astype(F32)
    x2 = r1.astype(BF16)
    x3 = (r1 - x2.astype(F32)).astype(BF16)
    return (jnp.dot(m01, x1, preferred_element_type=F32)
            + jnp.dot(m01, x2, preferred_element_type=F32)
            + jnp.dot(m01, x3, preferred_element_type=F32))


def _layer_kernel(layer, depth, is_last,
                  h_ref, nw_ref, wqkv_ref, wab_ref, wza_ref, whq_ref, whf_ref, whi_ref, wzb_ref,
                  wg_ref, cw_ref, alog_ref, dtb_ref, gnw_ref, lbl_ref, hnw_ref, wa_ref, wb_ref,
                  wo_ref, fnw_ref,
                  out_ref,
                  xn_s, qkv_s, q_s, k_s, v_s, ab_s, za_s, zb_s, hq_s, hf_s, hk_s, hi_s, ya_s, yb_s,
                  aqk_s, sg_s, sh_s):
    j = pl.program_id(1)

    @pl.when(j == 0)
    def _():
        qkv_s[0:SUBLANES, :] = jnp.zeros((SUBLANES, QKV), F32)
        sg_s[...] = jnp.zeros_like(sg_s)
        sh_s[...] = jnp.zeros_like(sh_s)

    h = h_ref[0]
    ms = jnp.mean(h * h, axis=-1, keepdims=True)
    xn_s[...] = (h * lax.rsqrt(ms + NORM_EPS) * nw_ref[...]).astype(BF16)

    qkv_s[SUBLANES:SUBLANES + TM, :] = jnp.dot(xn_s[...], wqkv_ref[...], preferred_element_type=F32)
    ab_s[...] = jnp.dot(xn_s[...], wab_ref[...], preferred_element_type=F32)
    za_s[...] = _silu(jnp.dot(xn_s[...], wza_ref[...], preferred_element_type=F32))
    zb_s[...] = _silu(jnp.dot(xn_s[...], wzb_ref[...], preferred_element_type=F32))
    hq_s[...] = jnp.dot(xn_s[...], whq_ref[...], preferred_element_type=F32)
    hi_s[...] = jnp.dot(xn_s[...], whi_ref[...], preferred_element_type=F32)

    lbl = lbl_ref[...]
    lbe = jnp.exp(lbl - jnp.max(lbl, axis=0, keepdims=True))
    lbp = lbe / jnp.sum(lbe, axis=0, keepdims=True)
    lb = jnp.zeros((1, BR), F32)
    for i in range(1, layer + 1):
        lb = lb + lbp[i:i + 1, :]
    f_logit = jnp.dot(xn_s[...], whf_ref[...], preferred_element_type=F32)
    e = jnp.exp(-jnp.abs(f_logit))
    r = 1.0 / (1.0 + e)
    pos = f_logit >= 0
    sig_p = jnp.where(pos, r, e * r)
    sig_n = jnp.where(pos, e * r, r)
    hf_s[...] = lb + (1.0 - lb) * sig_p
    hk_s[...] = (1.0 - lb) * sig_n

    for cb in range(QKV // LANES):
        cs = slice(cb * LANES, (cb + 1) * LANES)
        acc = cw_ref[0:1, cs] * qkv_s[SUBLANES - 3:SUBLANES - 3 + TM, cs]
        for t in range(1, GDN_CONV):
            acc = acc + cw_ref[t:t + 1, cs] * qkv_s[SUBLANES - 3 + t:SUBLANES - 3 + t + TM, cs]
        c = _silu(acc)
        if cb < 2 * HEADS:
            c = c * lax.rsqrt(jnp.sum(c * c, axis=-1, keepdims=True) + NORM_EPS)
            if cb < HEADS:
                q_s[:, cs] = c
            else:
                k_s[:, (cb - HEADS) * LANES:(cb - HEADS + 1) * LANES] = c
        else:
            v_s[:, (cb - 2 * HEADS) * LANES:(cb - 2 * HEADS + 1) * LANES] = c
    qkv_s[0:SUBLANES, :] = qkv_s[TM:TM + SUBLANES, :]

    row = lax.broadcasted_iota(jnp.int32, (GC, GC), 0)
    col = lax.broadcasted_iota(jnp.int32, (GC, GC), 1)
    causal = row >= col
    strict = row > col
    tril01 = causal.astype(BF16)
    eye = (row == col).astype(F32)
    rxc = row ^ col
    lev = jnp.full((GC, GC), -1, jnp.int32)
    for bit in range(GC_BITS):
        lev = lev + (rxc >= (1 << bit)).astype(jnp.int32)
    lev = jnp.where(strict, lev, -1)
    qscale = HD ** -0.5

    n_chunks = TM // GC
    chunk_rows = [slice(c * GC, (c + 1) * GC) for c in range(n_chunks)]
    head_cols = [slice(hd * HD, (hd + 1) * HD) for hd in range(HEADS)]
    pairs = [(c, hd) for c in range(n_chunks) for hd in range(HEADS)]

    gcum, eg_all, egd_all, beta_all, gcum_t = [], [], [], [], []
    for rows in chunk_rows:
        ab = ab_s[rows, :]
        x = ab + dtb_ref[...]
        softplus = jnp.maximum(x, 0.0) + jnp.log(1.0 + jnp.exp(-jnp.abs(x)))
        g = -jnp.exp(alog_ref[...]) * softplus
        beta_all.append(_sigmoid(ab))
        gc = _dot_01(tril01, g)
        gcum.append(gc)
        gcum_t.append(gc.T)
        eg_all.append(jnp.exp(gc))
        egd_all.append(jnp.exp(gc[GC - 1:GC, :] - gc))

    a_mats, rhs = [], []
    for c, hd in pairs:
        rows, hs = chunk_rows[c], head_cols[hd]
        k = k_s[rows, hs]
        beta = beta_all[c][:, HEADS + hd:HEADS + hd + 1]
        kb = k * beta
        qs = q_s[rows, hs] * qscale
        kq = _dot_nt(jnp.concatenate([kb.astype(BF16), qs.astype(BF16)], axis=0), k)
        diff = gcum[c][:, hd:hd + 1] - gcum_t[c][hd:hd + 1, :]
        lmat = jnp.where(causal, jnp.exp(jnp.minimum(diff, 0.0)), 0.0)
        a_mats.append(jnp.where(strict, kq[:GC] * lmat, 0.0))
        aqk_s[c * HEADS + hd] = (kq[GC:] * lmat).astype(BF16)
        rhs.append(jnp.concatenate([v_s[rows, hs] * beta, kb * eg_all[c][:, hd:hd + 1]],
                                   axis=1).astype(BF16))

    t_inv = [eye - jnp.where(lev == 0, a, 0.0) for a in a_mats]
    for l in range(1, GC_BITS):
        t_bf = [t.astype(BF16) for t in t_inv]
        et = [jnp.dot(jnp.where(lev == l, a, 0.0).astype(BF16), t, preferred_element_type=F32)
              for a, t in zip(a_mats, t_bf)]
        t_inv = [t - jnp.dot(tb, e.astype(BF16), preferred_element_type=F32)
                 for t, tb, e in zip(t_inv, t_bf, et)]
    uw = [jnp.dot(t.astype(BF16), r, preferred_element_type=F32) for t, r in zip(t_inv, rhs)]

    for c in range(n_chunks):
        rows = chunk_rows[c]
        wq, kd_t = [], []
        for hd in range(HEADS):
            hs = head_cols[hd]
            qg = q_s[rows, hs] * (qscale * eg_all[c][:, hd:hd + 1])
            w = uw[c * HEADS + hd][:, HD:]
            wq.append(_dot(jnp.concatenate([w.astype(BF16), qg.astype(BF16)], axis=0), sg_s[hd]))
            kd_t.append((k_s[rows, hs] * egd_all[c][:, hd:hd + 1]).T.astype(BF16))
        for hd in range(HEADS):
            hs = head_cols[hd]
            v_new = uw[c * HEADS + hd][:, :HD] - wq[hd][:GC]
            av = _dot(jnp.concatenate([aqk_s[c * HEADS + hd], kd_t[hd]], axis=0), v_new)
            o = wq[hd][GC:] + av[:GC]
            sg_s[hd] = eg_all[c][GC - 1:GC, hd:hd + 1] * sg_s[hd] + av[GC:]
            on = o * lax.rsqrt(jnp.mean(o * o, axis=-1, keepdims=True) + NORM_EPS) * gnw_ref[...]
            ya_s[rows, hs] = (on * za_s[rows, hs]).astype(BF16)

    assert HB == GC
    odd_row = (row & 1) == 1
    pos4 = row & 3
    for bi in range(TM // HB):
        rows = slice(bi * HB, (bi + 1) * HB)
        gsum = _dot_01(tril01, jnp.log(hf_s[rows, :]))
        glast = gsum[HB - 1:HB, :]
        for hd in range(HEADS):
            hs = head_cols[hd]
            q = hq_s[rows, hs]
            f = hf_s[rows, hs]
            kk = hk_s[rows, hs]
            v = hi_s[rows, hs]
            gh = gsum[:, hs]
            f_prev = pltpu.roll(f, 1, axis=0)
            f_next = pltpu.roll(f, HB - 1, axis=0)
            fac1 = jnp.where(pos4 == 0, f_next, jnp.where(pos4 == 1, 1.0, jnp.where(pos4 == 2, f, f * f_prev)))
            xs = [jnp.where(odd_row, q * f, kk), jnp.where(pos4 >= 2, q, kk) * fac1]
            for l in range(2, GC_BITS):
                half = 1 << l
                g3 = gh.reshape(HB // (2 * half), 2 * half, HD)
                bound = jnp.broadcast_to(g3[:, half - 1:half, :], g3.shape).reshape(HB, HD)
                decay = jnp.exp(-jnp.abs(gh - bound))
                xs.append(jnp.where(((row >> l) & 1) == 1, q, kk) * decay)
            amat = jnp.where(row == col, _dot_nt(q, kk), 0.0)
            for l, x in enumerate(xs):
                xb = x.astype(BF16)
                amat = jnp.where(lev == l, _dot_nt(xb, xb), amat)
            st = sh_s[hd]
            o = _dot(amat, v) + _dot_nt(q * jnp.exp(gh), st)
            kd = kk * jnp.exp(glast[:, hs] - gh)
            sh_s[hd] = st * jnp.exp(glast[:, hs]) + _dot(v.T, kd)
            on = o * lax.rsqrt(jnp.mean(o * o, axis=-1, keepdims=True) + NORM_EPS) * hnw_ref[...]
            yb_s[rows, hs] = (on * zb_s[rows, hs]).astype(BF16)

    gate_a = _sigmoid(jnp.dot(xn_s[...], wg_ref[:, :D_MODEL], preferred_element_type=F32))
    merged = gate_a * jnp.dot(ya_s[...], wa_ref[...], preferred_element_type=F32)
    gate_b = _sigmoid(jnp.dot(xn_s[...], wg_ref[:, D_MODEL:], preferred_element_type=F32))
    merged = merged + gate_b * jnp.dot(yb_s[...], wb_ref[...], preferred_element_type=F32)
    out = h_ref[0] + jnp.dot(merged.astype(BF16), wo_ref[...], preferred_element_type=F32)
    if is_last:
        out = out * lax.rsqrt(jnp.mean(out * out, axis=-1, keepdims=True) + NORM_EPS) * fnw_ref[...]
    out_ref[0] = out


def _full(shape):
    nd = len(shape)
    return pl.BlockSpec(shape, lambda b, j: (0,) * nd, pipeline_mode=pl.Buffered(1))


def _layer(h, layer, depth, is_last, p):
    bsz, seq, _ = h.shape
    assert seq % TM == 0 and TM % GC == 0 and TM % HB == 0 and HB % HC == 0
    weights = [p["nw"], p["wqkv"], p["wab"], p["wza"], p["whq"], p["whf"], p["whi"], p["wzb"],
               p["wg"], p["cw"], p["alog"], p["dtb"], p["gnw"], p["lbl"], p["hnw"], p["wa"], p["wb"],
               p["wo"], p["fnw"]]
    tile = pl.BlockSpec((1, TM, D_MODEL), lambda b, j: (b, j, 0))
    scratch = [
        pltpu.VMEM((TM, D_MODEL), BF16),
        pltpu.VMEM((TM + SUBLANES, QKV), F32),
        pltpu.VMEM((TM, BR), F32),
        pltpu.VMEM((TM, BR), F32),
        pltpu.VMEM((TM, BR), F32),
        pltpu.VMEM((TM, LANES), F32),
        pltpu.VMEM((TM, BR), F32),
        pltpu.VMEM((TM, BR), F32),
        pltpu.VMEM((TM, BR), F32),
        pltpu.VMEM((TM, BR), F32),
        pltpu.VMEM((TM, BR), F32),
        pltpu.VMEM((TM, BR), F32),
        pltpu.VMEM((TM, BR), BF16),
        pltpu.VMEM((TM, BR), BF16),
        pltpu.VMEM((TM // GC * HEADS, GC, GC), BF16),
        pltpu.VMEM((HEADS, HD, HD), F32),
        pltpu.VMEM((HEADS, HD, HD), F32),
    ]
    return pl.pallas_call(
        functools.partial(_layer_kernel, layer, depth, is_last),
        grid=(bsz, seq // TM),
        in_specs=[tile] + [_full(w.shape) for w in weights],
        out_specs=tile,
        out_shape=jax.ShapeDtypeStruct(h.shape, F32),
        scratch_shapes=scratch,
        compiler_params=pltpu.CompilerParams(
            dimension_semantics=("arbitrary", "arbitrary"), vmem_limit_bytes=VMEM_LIMIT),
        name=f"hybrid_layer{layer}",
    )(h, *weights)


def kernel(x, norm_w, w_in, conv_w, a_log, dt_bias, gdn_norm_w, hg_lb_logits, hg_norm_w, w_branch,
           w_out, final_norm_w):
    depth = w_in.shape[0]
    o_ab = QKV
    o_za = o_ab + 2 * HEADS
    o_hq = o_za + BR
    o_hf = o_hq + BR
    o_hi = o_hf + BR
    o_zb = o_hi + BR
    o_g = o_zb + BR

    def pad_lanes(t):
        return jnp.pad(t, ((0, 0), (0, LANES - t.shape[1])))

    h = x
    for l in range(depth):
        wl = w_in[l].astype(BF16)
        p = {
            "nw": norm_w[l][None, :],
            "wqkv": wl[:, :o_ab],
            "wab": pad_lanes(wl[:, o_ab:o_za]),
            "wza": wl[:, o_za:o_hq],
            "whq": wl[:, o_hq:o_hf],
            "whf": wl[:, o_hf:o_hi],
            "whi": wl[:, o_hi:o_zb],
            "wzb": wl[:, o_zb:o_g],
            "wg": wl[:, o_g:],
            "cw": conv_w[l],
            "alog": pad_lanes(a_log[l][None, :]),
            "dtb": pad_lanes(dt_bias[l][None, :]),
            "gnw": gdn_norm_w[l][None, :],
            "lbl": hg_lb_logits,
            "hnw": hg_norm_w[l][None, :],
            "wa": w_branch[l, 0].astype(BF16),
            "wb": w_branch[l, 1].astype(BF16),
            "wo": w_out[l].astype(BF16),
            "fnw": final_norm_w[None, :],
        }
        h = _layer(h, l, depth, l == depth - 1, p)
    return h
```

```python
import functools

import jax
import jax.numpy as jnp
import numpy as np
from jax import lax
from jax.experimental import pallas as pl
from jax.experimental.pallas import tpu as pltpu

F32 = jnp.float32
BF16 = jnp.bfloat16

D_MODEL = 1024
HEADS = 4
HD = 128
BR = HEADS * HD
QKV = 3 * BR
GDN_CONV = 4
NORM_EPS = 1e-6
LANES = 128
SUBLANES = 8

TM = 512
GC = 128
GC_BITS = 7
HB = GC
VMEM_LIMIT = 56 * 1024 * 1024


def _sigmoid(x):
    return 1.0 / (1.0 + jnp.exp(-x))


def _silu(x):
    return x * _sigmoid(x)


def _dot(a, b):
    return jnp.dot(a.astype(BF16), b.astype(BF16), preferred_element_type=F32)


def _dot_nt(a, b):
    return lax.dot_general(a.astype(BF16), b.astype(BF16), (((1,), (1,)), ((), ())),
                           preferred_element_type=F32)


def _dot_01(m01, x):
    x1 = x.astype(BF16)
    r1 = x - x1.astype(F32)
    x2 = r1.astype(BF16)
    x3 = (r1 - x2.astype(F32)).astype(BF16)
    return (jnp.dot(m01, x1, preferred_element_type=F32)
            + jnp.dot(m01, x2, preferred_element_type=F32)
            + jnp.dot(m01, x3, preferred_element_type=F32))


def _dot_01t(x, m01):
    x1 = x.astype(BF16)
    r1 = x - x1.astype(F32)
    x2 = r1.astype(BF16)
    x3 = (r1 - x2.astype(F32)).astype(BF16)
    return (jnp.dot(x1, m01, preferred_element_type=F32)
            + jnp.dot(x2, m01, preferred_element_type=F32)
            + jnp.dot(x3, m01, preferred_element_type=F32))


def _interleave(*gens):
    gens = list(gens)
    total = [float(next(g)) for g in gens]
    spent = [0.0] * len(gens)
    while gens:
        frac = [s / t for s, t in zip(spent, total)]
        i = frac.index(min(frac))
        try:
            spent[i] += next(gens[i])
        except StopIteration:
            for lst in (gens, total, spent):
                lst.pop(i)


def _layer_kernel(layer, depth, is_last,
                  h_ref, lev_ref, nw_ref, wqkv_ref, wab_ref, wza_ref, whq_ref, whf_ref, whi_ref, wzb_ref,
                  wg_ref, cw_ref, alog_ref, dtb_ref, gnw_ref, lbl_ref, hnw_ref, wa_ref, wb_ref,
                  wo_ref, fnw_ref,
                  out_ref,
                  xn_s, qkv_s, q_s, k_s, v_s, ab_s, za_s, zb_s, hq_s, hf_s, hk_s, hi_s, gate_s, ya_s, yb_s,
                  aqk_s, sg_s, sh_s):
    j = pl.program_id(1)

    @pl.when(j == 0)
    def _():
        qkv_s[0:SUBLANES, :] = jnp.zeros((SUBLANES, QKV), F32)
        sg_s[...] = jnp.zeros_like(sg_s)
        sh_s[...] = jnp.zeros_like(sh_s)

    h = h_ref[0]
    ms = jnp.mean(h * h, axis=-1, keepdims=True)
    xn_s[...] = (h * lax.rsqrt(ms + NORM_EPS) * nw_ref[...]).astype(BF16)

    qkv_s[SUBLANES:SUBLANES + TM, :] = jnp.dot(xn_s[...], wqkv_ref[...], preferred_element_type=F32)
    ab_s[...] = jnp.dot(xn_s[...], wab_ref[...], preferred_element_type=F32)
    row = lax.broadcasted_iota(jnp.int32, (GC, GC), 0)
    col = lax.broadcasted_iota(jnp.int32, (GC, GC), 1)
    causal = row >= col
    strict = row > col
    tril01 = causal.astype(BF16)
    triu01 = (row <= col).astype(BF16)
    eye = (row == col).astype(F32)
    lev = lev_ref[...]
    n_chunks = TM // GC
    chunk_rows = [slice(c * GC, (c + 1) * GC) for c in range(n_chunks)]

    col_all, gcum_t = [], []
    for rows in chunk_rows:
        ab_t = ab_s[rows, :].T[0:SUBLANES, :]
        x = ab_t + dtb_ref[...]
        softplus = jnp.maximum(x, 0.0) + jnp.log(1.0 + jnp.exp(-jnp.abs(x)))
        g_t = -jnp.exp(alog_ref[...]) * softplus
        gc_t = _dot_01t(g_t, triu01)
        gcum_t.append(gc_t)
        eg_t = jnp.exp(gc_t)
        egd_t = jnp.exp(gc_t[:, GC - 1:GC] - gc_t)
        head_row = lax.broadcasted_iota(jnp.int32, (SUBLANES, GC), 0) < HEADS
        packed = jnp.concatenate(
            [jnp.where(head_row, gc_t, _sigmoid(ab_t)), eg_t, egd_t,
             jnp.zeros((GC - 3 * SUBLANES, GC), F32)], axis=0)
        col_all.append(packed.T)

    za_s[...] = _silu(jnp.dot(xn_s[...], wza_ref[...], preferred_element_type=F32))
    zb_s[...] = _silu(jnp.dot(xn_s[...], wzb_ref[...], preferred_element_type=F32))
    hq_s[...] = jnp.dot(xn_s[...], whq_ref[...], preferred_element_type=F32)
    hi_s[...] = jnp.dot(xn_s[...], whi_ref[...], preferred_element_type=F32)

    lbl = lbl_ref[...]
    lbe = jnp.exp(lbl - jnp.max(lbl, axis=0, keepdims=True))
    lbp = lbe / jnp.sum(lbe, axis=0, keepdims=True)
    lb = jnp.zeros((1, BR), F32)
    for i in range(1, layer + 1):
        lb = lb + lbp[i:i + 1, :]
    f_logit = jnp.dot(xn_s[...], whf_ref[...], preferred_element_type=F32)
    e = jnp.exp(-jnp.abs(f_logit))
    r = 1.0 / (1.0 + e)
    pos = f_logit >= 0
    sig_p = jnp.where(pos, r, e * r)
    sig_n = jnp.where(pos, e * r, r)
    hf_s[...] = lb + (1.0 - lb) * sig_p
    hk_s[...] = (1.0 - lb) * sig_n

    def conv():
        yield (QKV // LANES) * 160
        for cb in range(QKV // LANES):
            cs = slice(cb * LANES, (cb + 1) * LANES)
            acc = cw_ref[0:1, cs] * qkv_s[SUBLANES - 3:SUBLANES - 3 + TM, cs]
            for t in range(1, GDN_CONV):
                acc = acc + cw_ref[t:t + 1, cs] * qkv_s[SUBLANES - 3 + t:SUBLANES - 3 + t + TM, cs]
            c = _silu(acc)
            if cb < 2 * HEADS:
                c = c * lax.rsqrt(jnp.sum(c * c, axis=-1, keepdims=True) + NORM_EPS)
                if cb < HEADS:
                    q_s[:, cs] = c
                else:
                    k_s[:, (cb - HEADS) * LANES:(cb - HEADS + 1) * LANES] = c
            else:
                v_s[:, (cb - 2 * HEADS) * LANES:(cb - 2 * HEADS + 1) * LANES] = c
            yield 160

    def gates():
        yield (D_MODEL // LANES) * 260
        for gb in range(2 * D_MODEL // (2 * LANES)):
            cs = slice(gb * 2 * LANES, (gb + 1) * 2 * LANES)
            gate_s[:, cs] = _sigmoid(jnp.dot(xn_s[...], wg_ref[:, cs], preferred_element_type=F32))
            yield 260

    _interleave(conv(), gates())
    qkv_s[0:SUBLANES, :] = qkv_s[TM:TM + SUBLANES, :]

    qscale = HD ** -0.5

    head_cols = [slice(hd * HD, (hd + 1) * HD) for hd in range(HEADS)]
    pairs = [(c, hd) for c in range(n_chunks) for hd in range(HEADS)]

    def gdn():
        yield len(pairs) * 120 + 2 * (GC_BITS - 1) * 300 + 300 + n_chunks * 600
        a_mats, rhs = [], []
        for c, hd in pairs:
            rows, hs = chunk_rows[c], head_cols[hd]
            k = k_s[rows, hs]
            beta = col_all[c][:, HEADS + hd:HEADS + hd + 1]
            kb = k * beta
            qs = q_s[rows, hs] * qscale
            kq = _dot_nt(jnp.concatenate([kb.astype(BF16), qs.astype(BF16)], axis=0), k)
            diff = col_all[c][:, hd:hd + 1] - gcum_t[c][hd:hd + 1, :]
            lmat = jnp.where(causal, jnp.exp(jnp.minimum(diff, 0.0)), 0.0)
            a_mats.append(jnp.where(strict, kq[:GC] * lmat, 0.0))
            aqk_s[c * HEADS + hd] = (kq[GC:] * lmat).astype(BF16)
            rhs.append(jnp.concatenate([v_s[rows, hs] * beta, kb * col_all[c][:, SUBLANES + hd:SUBLANES + hd + 1]],
                                       axis=1).astype(BF16))
            yield 120

        t_inv = [eye - jnp.where(lev == 0, a, 0.0) for a in a_mats]
        for l in range(1, GC_BITS):
            t_bf = [t.astype(BF16) for t in t_inv]
            et = [jnp.dot(jnp.where(lev == l, a, 0.0).astype(BF16), t, preferred_element_type=F32)
                  for a, t in zip(a_mats, t_bf)]
            yield 300
            t_inv = [t - jnp.dot(tb, e.astype(BF16), preferred_element_type=F32)
                     for t, tb, e in zip(t_inv, t_bf, et)]
            yield 300
        uw = [jnp.dot(t.astype(BF16), r, preferred_element_type=F32) for t, r in zip(t_inv, rhs)]
        yield 300

        for c in range(n_chunks):
            rows = chunk_rows[c]
            wq, kd_t = [], []
            for hd in range(HEADS):
                hs = head_cols[hd]
                qg = q_s[rows, hs] * (qscale * col_all[c][:, SUBLANES + hd:SUBLANES + hd + 1])
                w = uw[c * HEADS + hd][:, HD:]
                wq.append(_dot(jnp.concatenate([w.astype(BF16), qg.astype(BF16)], axis=0), sg_s[hd]))
                kd_t.append((k_s[rows, hs] * col_all[c][:, 2 * SUBLANES + hd:2 * SUBLANES + hd + 1]).T.astype(BF16))
            yield 300
            for hd in range(HEADS):
                hs = head_cols[hd]
                v_new = uw[c * HEADS + hd][:, :HD] - wq[hd][:GC]
                av = _dot(jnp.concatenate([aqk_s[c * HEADS + hd], kd_t[hd]], axis=0), v_new)
                o = wq[hd][GC:] + av[:GC]
                sg_s[hd] = col_all[c][GC - 1:GC, SUBLANES + hd:SUBLANES + hd + 1] * sg_s[hd] + av[GC:]
                on = o * lax.rsqrt(jnp.mean(o * o, axis=-1, keepdims=True) + NORM_EPS) * gnw_ref[...]
                ya_s[rows, hs] = (on * za_s[rows, hs]).astype(BF16)
            yield 300

    def hgrn():
        yield n_chunks * (250 + HEADS * 600)
        assert HB == GC
        odd_row = (row & 1) == 1
        pos4 = row & 3
        for bi in range(TM // HB):
            rows = slice(bi * HB, (bi + 1) * HB)
            gsum = _dot_01(tril01, jnp.log(hf_s[rows, :]))
            glast = gsum[HB - 1:HB, :]
            yield 250
            for hd in range(HEADS):
                hs = head_cols[hd]
                q = hq_s[rows, hs]
                f = hf_s[rows, hs]
                kk = hk_s[rows, hs]
                v = hi_s[rows, hs]
                gh = gsum[:, hs]
                f_prev = pltpu.roll(f, 1, axis=0)
                f_next = pltpu.roll(f, HB - 1, axis=0)
                fac1 = jnp.where(pos4 == 0, f_next, jnp.where(pos4 == 1, 1.0, jnp.where(pos4 == 2, f, f * f_prev)))
                xs = [jnp.where(odd_row, q * f, kk), jnp.where(pos4 >= 2, q, kk) * fac1]
                for l in range(2, GC_BITS):
                    half = 1 << l
                    g3 = gh.reshape(HB // (2 * half), 2 * half, HD)
                    bound = jnp.broadcast_to(g3[:, half - 1:half, :], g3.shape).reshape(HB, HD)
                    decay = jnp.exp(-jnp.abs(gh - bound))
                    xs.append(jnp.where(((row >> l) & 1) == 1, q, kk) * decay)
                yield 250
                amat = jnp.where(row == col, _dot_nt(q, kk), 0.0)
                for l, x in enumerate(xs):
                    xb = x.astype(BF16)
                    amat = jnp.where(lev == l, _dot_nt(xb, xb), amat)
                st = sh_s[hd]
                o = _dot(amat, v) + _dot_nt(q * jnp.exp(gh), st)
                kd = kk * jnp.exp(glast[:, hs] - gh)
                sh_s[hd] = st * jnp.exp(glast[:, hs]) + _dot(v.T, kd)
                on = o * lax.rsqrt(jnp.mean(o * o, axis=-1, keepdims=True) + NORM_EPS) * hnw_ref[...]
                yb_s[rows, hs] = (on * zb_s[rows, hs]).astype(BF16)
                yield 350

    _interleave(gdn(), hgrn())

    merged = gate_s[:, :D_MODEL] * jnp.dot(ya_s[...], wa_ref[...], preferred_element_type=F32)
    merged = merged + gate_s[:, D_MODEL:] * jnp.dot(yb_s[...], wb_ref[...], preferred_element_type=F32)
    out = h_ref[0] + jnp.dot(merged.astype(BF16), wo_ref[...], preferred_element_type=F32)
    if is_last:
        out = out * lax.rsqrt(jnp.mean(out * out, axis=-1, keepdims=True) + NORM_EPS) * fnw_ref[...]
    out_ref[0] = out


def _level_matrix():
    i = np.arange(GC)[:, None]
    j = np.arange(GC)[None, :]
    lev = np.floor(np.log2(np.maximum(i ^ j, 1))).astype(np.int32)
    return jnp.asarray(np.where(i > j, lev, -1).astype(np.int32))


def _full(shape):
    nd = len(shape)
    return pl.BlockSpec(shape, lambda b, j: (0,) * nd, pipeline_mode=pl.Buffered(1))


def _layer(h, layer, depth, is_last, p):
    bsz, seq, _ = h.shape
    assert seq % TM == 0 and TM % GC == 0 and GC == 1 << GC_BITS
    weights = [p["lev"], p["nw"], p["wqkv"], p["wab"], p["wza"], p["whq"], p["whf"], p["whi"], p["wzb"],
               p["wg"], p["cw"], p["alog"], p["dtb"], p["gnw"], p["lbl"], p["hnw"], p["wa"], p["wb"],
               p["wo"], p["fnw"]]
    tile = pl.BlockSpec((1, TM, D_MODEL), lambda b, j: (b, j, 0))
    scratch = [
        pltpu.VMEM((TM, D_MODEL), BF16),
        pltpu.VMEM((TM + SUBLANES, QKV), F32),
        pltpu.VMEM((TM, BR), F32),
        pltpu.VMEM((TM, BR), F32),
        pltpu.VMEM((TM, BR), F32),
        pltpu.VMEM((TM, LANES), F32),
        pltpu.VMEM((TM, BR), F32),
        pltpu.VMEM((TM, BR), F32),
        pltpu.VMEM((TM, BR), F32),
        pltpu.VMEM((TM, BR), F32),
        pltpu.VMEM((TM, BR), F32),
        pltpu.VMEM((TM, BR), F32),
        pltpu.VMEM((TM, 2 * D_MODEL), F32),
        pltpu.VMEM((TM, BR), BF16),
        pltpu.VMEM((TM, BR), BF16),
        pltpu.VMEM((TM // GC * HEADS, GC, GC), BF16),
        pltpu.VMEM((HEADS, HD, HD), F32),
        pltpu.VMEM((HEADS, HD, HD), F32),
    ]
    return pl.pallas_call(
        functools.partial(_layer_kernel, layer, depth, is_last),
        grid=(bsz, seq // TM),
        in_specs=[tile] + [_full(w.shape) for w in weights],
        out_specs=tile,
        out_shape=jax.ShapeDtypeStruct(h.shape, F32),
        scratch_shapes=scratch,
        compiler_params=pltpu.CompilerParams(
            dimension_semantics=("arbitrary", "arbitrary"), vmem_limit_bytes=VMEM_LIMIT),
        name=f"hybrid_layer{layer}",
    )(h, *weights)


def kernel(x, norm_w, w_in, conv_w, a_log, dt_bias, gdn_norm_w, hg_lb_logits, hg_norm_w, w_branch,
           w_out, final_norm_w):
    depth = w_in.shape[0]
    o_ab = QKV
    o_za = o_ab + 2 * HEADS
    o_hq = o_za + BR
    o_hf = o_hq + BR
    o_hi = o_hf + BR
    o_zb = o_hi + BR
    o_g = o_zb + BR

    def pad_lanes(t):
        return jnp.pad(t, ((0, 0), (0, LANES - t.shape[1])))

    def head_rows(t):
        return jnp.broadcast_to(jnp.pad(t, (0, SUBLANES - HEADS))[:, None], (SUBLANES, GC))

    lev = _level_matrix()
    h = x
    for l in range(depth):
        wl = w_in[l].astype(BF16)
        p = {
            "lev": lev,
            "nw": norm_w[l][None, :],
            "wqkv": wl[:, :o_ab],
            "wab": pad_lanes(wl[:, o_ab:o_za]),
            "wza": wl[:, o_za:o_hq],
            "whq": wl[:, o_hq:o_hf],
            "whf": wl[:, o_hf:o_hi],
            "whi": wl[:, o_hi:o_zb],
            "wzb": wl[:, o_zb:o_g],
            "wg": wl[:, o_g:],
            "cw": conv_w[l],
            "alog": head_rows(a_log[l]),
            "dtb": head_rows(dt_bias[l]),
            "gnw": gdn_norm_w[l][None, :],
            "lbl": hg_lb_logits,
            "hnw": hg_norm_w[l][None, :],
            "wa": w_branch[l, 0].astype(BF16),
            "wb": w_branch[l, 1].astype(BF16),
            "wo": w_out[l].astype(BF16),
            "fnw": final_norm_w[None, :],
        }
        h = _layer(h, l, depth, l == depth - 1, p)
    return h
```

```python
import functools

import jax
import jax.numpy as jnp
import numpy as np
from jax import lax
from jax.experimental import pallas as pl
from jax.experimental.pallas import tpu as pltpu

F32 = jnp.float32
BF16 = jnp.bfloat16

D_MODEL = 1024
HEADS = 4
HD = 128
BR = HEADS * HD
QKV = 3 * BR
GDN_CONV = 4
NORM_EPS = 1e-6
LANES = 128
SUBLANES = 8

TM = 512
GC = 128
GC_BITS = 7
HB = GC
VMEM_LIMIT = 56 * 1024 * 1024


def _sigmoid(x):
    return 1.0 / (1.0 + jnp.exp(-x))


def _silu(x):
    return x * _sigmoid(x)


def _dot(a, b):
    return jnp.dot(a.astype(BF16), b.astype(BF16), preferred_element_type=F32)


def _dot_nt(a, b):
    return lax.dot_general(a.astype(BF16), b.astype(BF16), (((1,), (1,)), ((), ())),
                           preferred_element_type=F32)


def _dot_01(m01, x):
    x1 = x.astype(BF16)
    r1 = x - x1.astype(F32)
    x2 = r1.astype(BF16)
    x3 = (r1 - x2.astype(F32)).astype(BF16)
    return (jnp.dot(m01, x1, preferred_element_type=F32)
            + jnp.dot(m01, x2, preferred_element_type=F32)
            + jnp.dot(m01, x3, preferred_element_type=F32))


def _dot_01t(x, m01):
    x1 = x.astype(BF16)
    r1 = x - x1.astype(F32)
    x2 = r1.astype(BF16)
    x3 = (r1 - x2.astype(F32)).astype(BF16)
    return (jnp.dot(x1, m01, preferred_element_type=F32)
            + jnp.dot(x2, m01, preferred_element_type=F32)
            + jnp.dot(x3, m01, preferred_element_type=F32))


def _interleave(*gens):
    gens = list(gens)
    total = [float(next(g)) for g in gens]
    spent = [0.0] * len(gens)
    while gens:
        frac = [s / t for s, t in zip(spent, total)]
        i = frac.index(min(frac))
        try:
            spent[i] += next(gens[i])
        except StopIteration:
            for lst in (gens, total, spent):
                lst.pop(i)


def _layer_kernel(layer, depth, is_last,
                  h_ref, lev_ref, nw_ref, wqkv_ref, wab_ref, wza_ref, whq_ref, whf_ref, whi_ref, wzb_ref,
                  wg_ref, cw_ref, alog_ref, dtb_ref, gnw_ref, lbl_ref, hnw_ref, wa_ref, wb_ref,
                  wo_ref, fnw_ref,
                  out_ref,
                  xn_s, qkv_s, q_s, k_s, v_s, ab_s, za_s, zb_s, hq_s, hf_s, hk_s, hi_s, gate_s, ya_s, yb_s,
                  aqk_s, sg_s, sh_s):
    j = pl.program_id(1)

    @pl.when(j == 0)
    def _():
        qkv_s[0:SUBLANES, :] = jnp.zeros((SUBLANES, QKV), F32)
        sg_s[...] = jnp.zeros_like(sg_s)
        sh_s[...] = jnp.zeros_like(sh_s)

    h = h_ref[0]
    ms = jnp.mean(h * h, axis=-1, keepdims=True)
    xn_s[...] = (h * lax.rsqrt(ms + NORM_EPS) * nw_ref[...]).astype(BF16)

    qkv_s[SUBLANES:SUBLANES + TM, :] = jnp.dot(xn_s[...], wqkv_ref[...], preferred_element_type=F32)
    ab_s[...] = jnp.dot(xn_s[...], wab_ref[...], preferred_element_type=F32)
    row = lax.broadcasted_iota(jnp.int32, (GC, GC), 0)
    col = lax.broadcasted_iota(jnp.int32, (GC, GC), 1)
    causal = row >= col
    strict = row > col
    tril01 = causal.astype(BF16)
    triu01 = (row <= col).astype(BF16)
    eye = (row == col).astype(F32)
    lev = lev_ref[...]
    n_chunks = TM // GC
    chunk_rows = [slice(c * GC, (c + 1) * GC) for c in range(n_chunks)]

    col_all, gcum_t = [], []
    for rows in chunk_rows:
        ab_t = ab_s[rows, :].T[0:SUBLANES, :]
        x = ab_t + dtb_ref[...]
        softplus = jnp.maximum(x, 0.0) + jnp.log(1.0 + jnp.exp(-jnp.abs(x)))
        g_t = -jnp.exp(alog_ref[...]) * softplus
        gc_t = _dot_01t(g_t, triu01)
        gcum_t.append(gc_t)
        eg_t = jnp.exp(gc_t)
        egd_t = jnp.exp(gc_t[:, GC - 1:GC] - gc_t)
        head_row = lax.broadcasted_iota(jnp.int32, (SUBLANES, GC), 0) < HEADS
        packed = jnp.concatenate(
            [jnp.where(head_row, gc_t, _sigmoid(ab_t)), eg_t, egd_t,
             jnp.zeros((GC - 3 * SUBLANES, GC), F32)], axis=0)
        col_all.append(packed.T)

    def conv():
        yield (QKV // LANES) * 160
        for cb in range(QKV // LANES):
            cs = slice(cb * LANES, (cb + 1) * LANES)
            acc = cw_ref[0:1, cs] * qkv_s[SUBLANES - 3:SUBLANES - 3 + TM, cs]
            for t in range(1, GDN_CONV):
                acc = acc + cw_ref[t:t + 1, cs] * qkv_s[SUBLANES - 3 + t:SUBLANES - 3 + t + TM, cs]
            c = _silu(acc)
            if cb < 2 * HEADS:
                c = c * lax.rsqrt(jnp.sum(c * c, axis=-1, keepdims=True) + NORM_EPS)
                if cb < HEADS:
                    q_s[:, cs] = c
                else:
                    k_s[:, (cb - HEADS) * LANES:(cb - HEADS + 1) * LANES] = c
            else:
                v_s[:, (cb - 2 * HEADS) * LANES:(cb - 2 * HEADS + 1) * LANES] = c
            yield 160

    def proj_b():
        yield 700 + 4 * 550
        lbl = lbl_ref[...]
        lbe = jnp.exp(lbl - jnp.max(lbl, axis=0, keepdims=True))
        lbp = lbe / jnp.sum(lbe, axis=0, keepdims=True)
        lb = jnp.zeros((1, BR), F32)
        for i in range(1, layer + 1):
            lb = lb + lbp[i:i + 1, :]
        f_logit = jnp.dot(xn_s[...], whf_ref[...], preferred_element_type=F32)
        e = jnp.exp(-jnp.abs(f_logit))
        r = 1.0 / (1.0 + e)
        pos = f_logit >= 0
        hf_s[...] = lb + (1.0 - lb) * jnp.where(pos, r, e * r)
        hk_s[...] = (1.0 - lb) * jnp.where(pos, e * r, r)
        yield 700
        hq_s[...] = jnp.dot(xn_s[...], whq_ref[...], preferred_element_type=F32)
        yield 550
        hi_s[...] = jnp.dot(xn_s[...], whi_ref[...], preferred_element_type=F32)
        yield 550
        zb_s[...] = _silu(jnp.dot(xn_s[...], wzb_ref[...], preferred_element_type=F32))
        yield 550
        za_s[...] = _silu(jnp.dot(xn_s[...], wza_ref[...], preferred_element_type=F32))
        yield 550

    def gates():
        yield (D_MODEL // LANES) * 260
        for gb in range(2 * D_MODEL // (2 * LANES)):
            cs = slice(gb * 2 * LANES, (gb + 1) * 2 * LANES)
            gate_s[:, cs] = _sigmoid(jnp.dot(xn_s[...], wg_ref[:, cs], preferred_element_type=F32))
            yield 260

    _interleave(conv(), proj_b())
    qkv_s[0:SUBLANES, :] = qkv_s[TM:TM + SUBLANES, :]

    qscale = HD ** -0.5

    head_cols = [slice(hd * HD, (hd + 1) * HD) for hd in range(HEADS)]
    pairs = [(c, hd) for c in range(n_chunks) for hd in range(HEADS)]

    def gdn():
        yield len(pairs) * 120 + 2 * (GC_BITS - 1) * 300 + 300 + n_chunks * 600
        a_mats, rhs = [], []
        for c, hd in pairs:
            rows, hs = chunk_rows[c], head_cols[hd]
            k = k_s[rows, hs]
            beta = col_all[c][:, HEADS + hd:HEADS + hd + 1]
            kb = k * beta
            qs = q_s[rows, hs] * qscale
            kq = _dot_nt(jnp.concatenate([kb.astype(BF16), qs.astype(BF16)], axis=0), k)
            diff = col_all[c][:, hd:hd + 1] - gcum_t[c][hd:hd + 1, :]
            lmat = jnp.where(causal, jnp.exp(jnp.minimum(diff, 0.0)), 0.0)
            a_mats.append(jnp.where(strict, kq[:GC] * lmat, 0.0))
            aqk_s[c * HEADS + hd] = (kq[GC:] * lmat).astype(BF16)
            rhs.append(jnp.concatenate([v_s[rows, hs] * beta, kb * col_all[c][:, SUBLANES + hd:SUBLANES + hd + 1]],
                                       axis=1).astype(BF16))
            yield 120

        t_inv = [eye - jnp.where(lev == 0, a, 0.0) for a in a_mats]
        for l in range(1, GC_BITS):
            t_bf = [t.astype(BF16) for t in t_inv]
            et = [jnp.dot(jnp.where(lev == l, a, 0.0).astype(BF16), t, preferred_element_type=F32)
                  for a, t in zip(a_mats, t_bf)]
            yield 300
            t_inv = [t - jnp.dot(tb, e.astype(BF16), preferred_element_type=F32)
                     for t, tb, e in zip(t_inv, t_bf, et)]
            yield 300
        uw = [jnp.dot(t.astype(BF16), r, preferred_element_type=F32) for t, r in zip(t_inv, rhs)]
        yield 300

        for c in range(n_chunks):
            rows = chunk_rows[c]
            wq, kd_t = [], []
            for hd in range(HEADS):
                hs = head_cols[hd]
                qg = q_s[rows, hs] * (qscale * col_all[c][:, SUBLANES + hd:SUBLANES + hd + 1])
                w = uw[c * HEADS + hd][:, HD:]
                wq.append(_dot(jnp.concatenate([w.astype(BF16), qg.astype(BF16)], axis=0), sg_s[hd]))
                kd_t.append((k_s[rows, hs] * col_all[c][:, 2 * SUBLANES + hd:2 * SUBLANES + hd + 1]).T.astype(BF16))
            yield 300
            for hd in range(HEADS):
                hs = head_cols[hd]
                v_new = uw[c * HEADS + hd][:, :HD] - wq[hd][:GC]
                av = _dot(jnp.concatenate([aqk_s[c * HEADS + hd], kd_t[hd]], axis=0), v_new)
                o = wq[hd][GC:] + av[:GC]
                sg_s[hd] = col_all[c][GC - 1:GC, SUBLANES + hd:SUBLANES + hd + 1] * sg_s[hd] + av[GC:]
                on = o * lax.rsqrt(jnp.mean(o * o, axis=-1, keepdims=True) + NORM_EPS) * gnw_ref[...]
                ya_s[rows, hs] = (on * za_s[rows, hs]).astype(BF16)
            yield 300

    def hgrn():
        yield n_chunks * (250 + HEADS * 600)
        assert HB == GC
        odd_row = (row & 1) == 1
        pos4 = row & 3
        for bi in range(TM // HB):
            rows = slice(bi * HB, (bi + 1) * HB)
            gsum = _dot_01(tril01, jnp.log(hf_s[rows, :]))
            glast = gsum[HB - 1:HB, :]
            yield 250
            for hd in range(HEADS):
                hs = head_cols[hd]
                q = hq_s[rows, hs]
                f = hf_s[rows, hs]
                kk = hk_s[rows, hs]
                v = hi_s[rows, hs]
                gh = gsum[:, hs]
                f_prev = pltpu.roll(f, 1, axis=0)
                f_next = pltpu.roll(f, HB - 1, axis=0)
                fac1 = jnp.where(pos4 == 0, f_next, jnp.where(pos4 == 1, 1.0, jnp.where(pos4 == 2, f, f * f_prev)))
                xs = [jnp.where(odd_row, q * f, kk), jnp.where(pos4 >= 2, q, kk) * fac1]
                for l in range(2, GC_BITS):
                    half = 1 << l
                    g3 = gh.reshape(HB // (2 * half), 2 * half, HD)
                    bound = jnp.broadcast_to(g3[:, half - 1:half, :], g3.shape).reshape(HB, HD)
                    decay = jnp.exp(-jnp.abs(gh - bound))
                    xs.append(jnp.where(((row >> l) & 1) == 1, q, kk) * decay)
                yield 250
                amat = jnp.where(row == col, _dot_nt(q, kk), 0.0)
                for l, x in enumerate(xs):
                    xb = x.astype(BF16)
                    amat = jnp.where(lev == l, _dot_nt(xb, xb), amat)
                st = sh_s[hd]
                o = _dot(amat, v) + _dot_nt(q * jnp.exp(gh), st)
                kd = kk * jnp.exp(glast[:, hs] - gh)
                sh_s[hd] = st * jnp.exp(glast[:, hs]) + _dot(v.T, kd)
                on = o * lax.rsqrt(jnp.mean(o * o, axis=-1, keepdims=True) + NORM_EPS) * hnw_ref[...]
                yb_s[rows, hs] = (on * zb_s[rows, hs]).astype(BF16)
                yield 350

    _interleave(gdn(), hgrn(), gates())

    merged = gate_s[:, :D_MODEL] * jnp.dot(ya_s[...], wa_ref[...], preferred_element_type=F32)
    merged = merged + gate_s[:, D_MODEL:] * jnp.dot(yb_s[...], wb_ref[...], preferred_element_type=F32)
    out = h_ref[0] + jnp.dot(merged.astype(BF16), wo_ref[...], preferred_element_type=F32)
    if is_last:
        out = out * lax.rsqrt(jnp.mean(out * out, axis=-1, keepdims=True) + NORM_EPS) * fnw_ref[...]
    out_ref[0] = out


def _level_matrix():
    i = np.arange(GC)[:, None]
    j = np.arange(GC)[None, :]
    lev = np.floor(np.log2(np.maximum(i ^ j, 1))).astype(np.int32)
    return jnp.asarray(np.where(i > j, lev, -1).astype(np.int32))


def _full(shape):
    nd = len(shape)
    return pl.BlockSpec(shape, lambda b, j: (0,) * nd, pipeline_mode=pl.Buffered(1))


def _layer(h, layer, depth, is_last, p):
    bsz, seq, _ = h.shape
    assert seq % TM == 0 and TM % GC == 0 and GC == 1 << GC_BITS
    weights = [p["lev"], p["nw"], p["wqkv"], p["wab"], p["wza"], p["whq"], p["whf"], p["whi"], p["wzb"],
               p["wg"], p["cw"], p["alog"], p["dtb"], p["gnw"], p["lbl"], p["hnw"], p["wa"], p["wb"],
               p["wo"], p["fnw"]]
    tile = pl.BlockSpec((1, TM, D_MODEL), lambda b, j: (b, j, 0))
    scratch = [
        pltpu.VMEM((TM, D_MODEL), BF16),
        pltpu.VMEM((TM + SUBLANES, QKV), F32),
        pltpu.VMEM((TM, BR), F32),
        pltpu.VMEM((TM, BR), F32),
        pltpu.VMEM((TM, BR), F32),
        pltpu.VMEM((TM, LANES), F32),
        pltpu.VMEM((TM, BR), F32),
        pltpu.VMEM((TM, BR), F32),
        pltpu.VMEM((TM, BR), F32),
        pltpu.VMEM((TM, BR), F32),
        pltpu.VMEM((TM, BR), F32),
        pltpu.VMEM((TM, BR), F32),
        pltpu.VMEM((TM, 2 * D_MODEL), F32),
        pltpu.VMEM((TM, BR), BF16),
        pltpu.VMEM((TM, BR), BF16),
        pltpu.VMEM((TM // GC * HEADS, GC, GC), BF16),
        pltpu.VMEM((HEADS, HD, HD), F32),
        pltpu.VMEM((HEADS, HD, HD), F32),
    ]
    return pl.pallas_call(
        functools.partial(_layer_kernel, layer, depth, is_last),
        grid=(bsz, seq // TM),
        in_specs=[tile] + [_full(w.shape) for w in weights],
        out_specs=tile,
        out_shape=jax.ShapeDtypeStruct(h.shape, F32),
        scratch_shapes=scratch,
        compiler_params=pltpu.CompilerParams(
            dimension_semantics=("arbitrary", "arbitrary"), vmem_limit_bytes=VMEM_LIMIT),
        name=f"hybrid_layer{layer}",
    )(h, *weights)


def kernel(x, norm_w, w_in, conv_w, a_log, dt_bias, gdn_norm_w, hg_lb_logits, hg_norm_w, w_branch,
           w_out, final_norm_w):
    depth = w_in.shape[0]
    o_ab = QKV
    o_za = o_ab + 2 * HEADS
    o_hq = o_za + BR
    o_hf = o_hq + BR
    o_hi = o_hf + BR
    o_zb = o_hi + BR
    o_g = o_zb + BR

    def pad_lanes(t):
        return jnp.pad(t, ((0, 0), (0, LANES - t.shape[1])))

    def head_rows(t):
        return jnp.broadcast_to(jnp.pad(t, (0, SUBLANES - HEADS))[:, None], (SUBLANES, GC))

    lev = _level_matrix()
    h = x
    for l in range(depth):
        wl = w_in[l].astype(BF16)
        p = {
            "lev": lev,
            "nw": norm_w[l][None, :],
            "wqkv": wl[:, :o_ab],
            "wab": pad_lanes(wl[:, o_ab:o_za]),
            "wza": wl[:, o_za:o_hq],
            "whq": wl[:, o_hq:o_hf],
            "whf": wl[:, o_hf:o_hi],
            "whi": wl[:, o_hi:o_zb],
            "wzb": wl[:, o_zb:o_g],
            "wg": wl[:, o_g:],
            "cw": conv_w[l],
            "alog": head_rows(a_log[l]),
            "dtb": head_rows(dt_bias[l]),
            "gnw": gdn_norm_w[l][None, :],
            "lbl": hg_lb_logits,
            "hnw": hg_norm_w[l][None, :],
            "wa": w_branch[l, 0].astype(BF16),
            "wb": w_branch[l, 1].astype(BF16),
            "wo": w_out[l].astype(BF16),
            "fnw": final_norm_w[None, :],
        }
        h = _layer(h, l, depth, l == depth - 1, p)
    return h
```

```python
import functools

import jax
import jax.numpy as jnp
import numpy as np
from jax import lax
from jax.experimental import pallas as pl
from jax.experimental.pallas import tpu as pltpu

F32 = jnp.float32
BF16 = jnp.bfloat16

D_MODEL = 1024
HEADS = 4
HD = 128
BR = HEADS * HD
QKV = 3 * BR
GDN_CONV = 4
NORM_EPS = 1e-6
LANES = 128
SUBLANES = 8

ROWS = 2
TS = 256
TM = ROWS * TS
GC = 128
GC_BITS = 7
HB = GC
CHUNKS_PER_ROW = TS // GC
VMEM_LIMIT = 56 * 1024 * 1024


def _sigmoid(x):
    return 1.0 / (1.0 + jnp.exp(-x))


def _silu(x):
    return x * _sigmoid(x)


def _dot(a, b):
    return jnp.dot(a.astype(BF16), b.astype(BF16), preferred_element_type=F32)


def _dot_nt(a, b):
    return lax.dot_general(a.astype(BF16), b.astype(BF16), (((1,), (1,)), ((), ())),
                           preferred_element_type=F32)


def _dot_01(m01, x):
    x1 = x.astype(BF16)
    r1 = x - x1.astype(F32)
    x2 = r1.astype(BF16)
    x3 = (r1 - x2.astype(F32)).astype(BF16)
    return (jnp.dot(m01, x1, preferred_element_type=F32)
            + jnp.dot(m01, x2, preferred_element_type=F32)
            + jnp.dot(m01, x3, preferred_element_type=F32))


def _dot_01t(x, m01):
    x1 = x.astype(BF16)
    r1 = x - x1.astype(F32)
    x2 = r1.astype(BF16)
    x3 = (r1 - x2.astype(F32)).astype(BF16)
    return (jnp.dot(x1, m01, preferred_element_type=F32)
            + jnp.dot(x2, m01, preferred_element_type=F32)
            + jnp.dot(x3, m01, preferred_element_type=F32))


def _interleave(*gens):
    gens = list(gens)
    total = [float(next(g)) for g in gens]
    spent = [0.0] * len(gens)
    while gens:
        frac = [s / t for s, t in zip(spent, total)]
        i = frac.index(min(frac))
        try:
            spent[i] += next(gens[i])
        except StopIteration:
            for lst in (gens, total, spent):
                lst.pop(i)


def _layer_kernel(layer, depth, is_last,
                  h_ref, lev_ref, nw_ref, wqkv_ref, wab_ref, wza_ref, whq_ref, whf_ref, whi_ref, wzb_ref,
                  wg_ref, cw_ref, alog_ref, dtb_ref, gnw_ref, lbl_ref, hnw_ref, wa_ref, wb_ref,
                  wo_ref, fnw_ref,
                  out_ref,
                  xn_s, qkv_s, q_s, k_s, v_s, ab_s, za_s, zb_s, hq_s, hf_s, hk_s, hi_s, gate_s, ya_s, yb_s,
                  aqk_s, sg_s, sh_s):
    j = pl.program_id(1)

    @pl.when(j == 0)
    def _():
        qkv_s[:, 0:SUBLANES, :] = jnp.zeros((ROWS, SUBLANES, QKV), F32)
        sg_s[...] = jnp.zeros_like(sg_s)
        sh_s[...] = jnp.zeros_like(sh_s)

    h = h_ref[...].reshape(TM, D_MODEL)
    ms =jnp.mean(h * h, axis=-1, keepdims=True)
    xn_s[...] = (h * lax.rsqrt(ms + NORM_EPS) * nw_ref[...]).astype(BF16)

    qkv = jnp.dot(xn_s[...], wqkv_ref[...], preferred_element_type=F32)
    for r in range(ROWS):
        qkv_s[r, SUBLANES:SUBLANES + TS, :] = qkv[r * TS:(r + 1) * TS, :]
    ab_s[...] = jnp.dot(xn_s[...], wab_ref[...], preferred_element_type=F32)
    row = lax.broadcasted_iota(jnp.int32, (GC, GC), 0)
    col = lax.broadcasted_iota(jnp.int32, (GC, GC), 1)
    causal = row >= col
    strict = row > col
    tril01 = causal.astype(BF16)
    triu01 = (row <= col).astype(BF16)
    eye = (row == col).astype(F32)
    lev = lev_ref[...]
    n_chunks = TM // GC
    chunk_rows = [slice(c * GC, (c + 1) * GC) for c in range(n_chunks)]

    col_all, gcum_t = [], []
    for rows in chunk_rows:
        ab_t = ab_s[rows, :].T[0:SUBLANES, :]
        x = ab_t + dtb_ref[...]
        softplus = jnp.maximum(x, 0.0) + jnp.log(1.0 + jnp.exp(-jnp.abs(x)))
        g_t = -jnp.exp(alog_ref[...]) * softplus
        gc_t = _dot_01t(g_t, triu01)
        gcum_t.append(gc_t)
        eg_t = jnp.exp(gc_t)
        egd_t = jnp.exp(gc_t[:, GC - 1:GC] - gc_t)
        head_row = lax.broadcasted_iota(jnp.int32, (SUBLANES, GC), 0) < HEADS
        packed = jnp.concatenate(
            [jnp.where(head_row, gc_t, _sigmoid(ab_t)), eg_t, egd_t,
             jnp.zeros((GC - 3 * SUBLANES, GC), F32)], axis=0)
        col_all.append(packed.T)

    def conv():
        yield ROWS * (QKV // LANES) * 80
        for r, cb in [(r, cb) for cb in range(QKV // LANES) for r in range(ROWS)]:
            cs = slice(cb * LANES, (cb + 1) * LANES)
            tok = slice(r * TS, (r + 1) * TS)
            acc = cw_ref[0:1, cs] * qkv_s[r, SUBLANES - 3:SUBLANES - 3 + TS, cs]
            for t in range(1, GDN_CONV):
                acc = acc + cw_ref[t:t + 1, cs] * qkv_s[r, SUBLANES - 3 + t:SUBLANES - 3 + t + TS, cs]
            c = _silu(acc)
            if cb < 2 * HEADS:
                c = c * lax.rsqrt(jnp.sum(c * c, axis=-1, keepdims=True) + NORM_EPS)
                if cb < HEADS:
                    q_s[tok, cs] = c
                else:
                    k_s[tok, (cb - HEADS) * LANES:(cb - HEADS + 1) * LANES] = c
            else:
                v_s[tok, (cb - 2 * HEADS) * LANES:(cb - 2 * HEADS + 1) * LANES] = c
            yield 80

    def proj_b():
        yield 700 + 4 * 550
        lbl = lbl_ref[...]
        lbe = jnp.exp(lbl - jnp.max(lbl, axis=0, keepdims=True))
        lbp = lbe / jnp.sum(lbe, axis=0, keepdims=True)
        lb = jnp.zeros((1, BR), F32)
        for i in range(1, layer + 1):
            lb = lb + lbp[i:i + 1, :]
        f_logit = jnp.dot(xn_s[...], whf_ref[...], preferred_element_type=F32)
        e = jnp.exp(-jnp.abs(f_logit))
        r = 1.0 / (1.0 + e)
        pos = f_logit >= 0
        hf_s[...] = lb + (1.0 - lb) * jnp.where(pos, r, e * r)
        hk_s[...] = (1.0 - lb) * jnp.where(pos, e * r, r)
        yield 700
        hq_s[...] = jnp.dot(xn_s[...], whq_ref[...], preferred_element_type=F32)
        yield 550
        hi_s[...] = jnp.dot(xn_s[...], whi_ref[...], preferred_element_type=F32)
        yield 550
        zb_s[...] = _silu(jnp.dot(xn_s[...], wzb_ref[...], preferred_element_type=F32))
        yield 550
        za_s[...] = _silu(jnp.dot(xn_s[...], wza_ref[...], preferred_element_type=F32))
        yield 550

    def gates():
        yield (D_MODEL // LANES) * 260
        for gb in range(2 * D_MODEL // (2 * LANES)):
            cs = slice(gb * 2 * LANES, (gb + 1) * 2 * LANES)
            gate_s[:, cs] = _sigmoid(jnp.dot(xn_s[...], wg_ref[:, cs], preferred_element_type=F32))
            yield 260

    _interleave(conv(), proj_b())
    qkv_s[:, 0:SUBLANES, :] = qkv_s[:, TS:TS + SUBLANES, :]

    qscale = HD ** -0.5

    head_cols = [slice(hd * HD, (hd + 1) * HD) for hd in range(HEADS)]
    pairs = [(c, hd) for c in range(n_chunks) for hd in range(HEADS)]

    def gdn():
        yield len(pairs) * 120 + 2 * (GC_BITS - 1) * 300 + 300 + n_chunks * 600
        a_mats, rhs = [], []
        for c, hd in pairs:
            rows, hs = chunk_rows[c], head_cols[hd]
            k = k_s[rows, hs]
            beta = col_all[c][:, HEADS + hd:HEADS + hd + 1]
            kb = k * beta
            qs = q_s[rows, hs] * qscale
            kq = _dot_nt(jnp.concatenate([kb.astype(BF16), qs.astype(BF16)], axis=0), k)
            diff = col_all[c][:, hd:hd + 1] - gcum_t[c][hd:hd + 1, :]
            lmat = jnp.where(causal, jnp.exp(jnp.minimum(diff, 0.0)), 0.0)
            a_mats.append(jnp.where(strict, kq[:GC] * lmat, 0.0))
            aqk_s[c * HEADS + hd] = (kq[GC:] * lmat).astype(BF16)
            rhs.append(jnp.concatenate([v_s[rows, hs] * beta, kb * col_all[c][:, SUBLANES + hd:SUBLANES + hd + 1]],
                                       axis=1).astype(BF16))
            yield 120

        t_inv = [eye - jnp.where(lev == 0, a, 0.0) for a in a_mats]
        for l in range(1, GC_BITS):
            t_bf = [t.astype(BF16) for t in t_inv]
            et = [jnp.dot(jnp.where(lev == l, a, 0.0).astype(BF16), t, preferred_element_type=F32)
                  for a, t in zip(a_mats, t_bf)]
            yield 300
            t_inv = [t - jnp.dot(tb, e.astype(BF16), preferred_element_type=F32)
                     for t, tb, e in zip(t_inv, t_bf, et)]
            yield 300
        uw = [jnp.dot(t.astype(BF16), r, preferred_element_type=F32) for t, r in zip(t_inv, rhs)]
        yield 300

        for step in range(CHUNKS_PER_ROW):
            group = [(r * CHUNKS_PER_ROW + step, r * HEADS + hd) for r in range(ROWS) for hd in range(HEADS)]
            wq, kd_t = [], []
            for c, si in group:
                rows, hd = chunk_rows[c], si % HEADS
                hs = head_cols[hd]
                qg = q_s[rows, hs] * (qscale * col_all[c][:, SUBLANES + hd:SUBLANES + hd + 1])
                w = uw[c * HEADS + hd][:, HD:]
                wq.append(_dot(jnp.concatenate([w.astype(BF16), qg.astype(BF16)], axis=0), sg_s[si]))
                kd_t.append((k_s[rows, hs] * col_all[c][:, 2 * SUBLANES + hd:2 * SUBLANES + hd + 1]).T.astype(BF16))
            yield 600
            for i, (c, si) in enumerate(group):
                rows, hd = chunk_rows[c], si % HEADS
                hs = head_cols[hd]
                v_new = uw[c * HEADS + hd][:, :HD] - wq[i][:GC]
                av = _dot(jnp.concatenate([aqk_s[c * HEADS + hd], kd_t[i]], axis=0), v_new)
                o = wq[i][GC:] + av[:GC]
                sg_s[si] = col_all[c][GC - 1:GC, SUBLANES + hd:SUBLANES + hd + 1] * sg_s[si] + av[GC:]
                on = o * lax.rsqrt(jnp.mean(o * o, axis=-1, keepdims=True) + NORM_EPS) * gnw_ref[...]
                ya_s[rows, hs] = (on * za_s[rows, hs]).astype(BF16)
            yield 600

    def hgrn():
        yield n_chunks * (250 + HEADS * 600)
        assert HB == GC
        odd_row = (row & 1) == 1
        pos4 = row & 3
        for bi, r in [(r * CHUNKS_PER_ROW + step, r) for step in range(CHUNKS_PER_ROW) for r in range(ROWS)]:
            rows = slice(bi * HB, (bi + 1) * HB)
            gsum = _dot_01(tril01, jnp.log(hf_s[rows, :]))
            glast = gsum[HB - 1:HB, :]
            yield 250
            for hd in range(HEADS):
                hs = head_cols[hd]
                q = hq_s[rows, hs]
                f = hf_s[rows, hs]
                kk = hk_s[rows, hs]
                v = hi_s[rows, hs]
                gh = gsum[:, hs]
                f_prev = pltpu.roll(f, 1, axis=0)
                f_next = pltpu.roll(f, HB - 1, axis=0)
                fac1 = jnp.where(pos4 == 0, f_next, jnp.where(pos4 == 1, 1.0, jnp.where(pos4 == 2, f, f * f_prev)))
                xs = [jnp.where(odd_row, q * f, kk), jnp.where(pos4 >= 2, q, kk) * fac1]
                for l in range(2, GC_BITS):
                    half = 1 << l
                    g3 = gh.reshape(HB // (2 * half), 2 * half, HD)
                    bound = jnp.broadcast_to(g3[:, half - 1:half, :], g3.shape).reshape(HB, HD)
                    decay = jnp.exp(-jnp.abs(gh - bound))
                    xs.append(jnp.where(((row >> l) & 1) == 1, q, kk) * decay)
                yield 250
                amat = jnp.where(row == col, _dot_nt(q, kk), 0.0)
                for l, x in enumerate(xs):
                    xb = x.astype(BF16)
                    amat = jnp.where(lev == l, _dot_nt(xb, xb), amat)
                st = sh_s[r * HEADS + hd]
                o = _dot(amat, v) + _dot_nt(q * jnp.exp(gh), st)
                kd = kk * jnp.exp(glast[:, hs] - gh)
                sh_s[r * HEADS + hd] = st * jnp.exp(glast[:, hs]) + _dot(v.T, kd)
                on = o * lax.rsqrt(jnp.mean(o * o, axis=-1, keepdims=True) + NORM_EPS) * hnw_ref[...]
                yb_s[rows, hs] = (on * zb_s[rows, hs]).astype(BF16)
                yield 350

    _interleave(gdn(), hgrn(), gates())

    merged = gate_s[:, :D_MODEL] * jnp.dot(ya_s[...], wa_ref[...], preferred_element_type=F32)
    merged = merged + gate_s[:, D_MODEL:] * jnp.dot(yb_s[...], wb_ref[...], preferred_element_type=F32)
    out = h_ref[...].reshape(TM, D_MODEL) + jnp.dot(merged.astype(BF16), wo_ref[...],
                                                     preferred_element_type=F32)
    if is_last:
        out = out * lax.rsqrt(jnp.mean(out * out, axis=-1, keepdims=True) + NORM_EPS) * fnw_ref[...]
    out_ref[...] = out.reshape(ROWS, TS, D_MODEL)


def _level_matrix():
    i = np.arange(GC)[:, None]
    j = np.arange(GC)[None, :]
    lev = np.floor(np.log2(np.maximum(i ^ j, 1))).astype(np.int32)
    return jnp.asarray(np.where(i > j, lev, -1).astype(np.int32))


def _full(shape):
    nd = len(shape)
    return pl.BlockSpec(shape, lambda b, j: (0,) * nd, pipeline_mode=pl.Buffered(1))


def _layer(h, layer, depth, is_last, p):
    bsz, seq, _ = h.shape
    assert seq % TS == 0 and bsz % ROWS == 0 and TS % GC == 0 and GC == 1 << GC_BITS
    weights = [p["lev"], p["nw"], p["wqkv"], p["wab"], p["wza"], p["whq"], p["whf"], p["whi"], p["wzb"],
               p["wg"], p["cw"], p["alog"], p["dtb"], p["gnw"], p["lbl"], p["hnw"], p["wa"], p["wb"],
               p["wo"], p["fnw"]]
    tile = pl.BlockSpec((ROWS, TS, D_MODEL), lambda b, j: (b, j, 0))
    scratch = [
        pltpu.VMEM((TM, D_MODEL), BF16),
        pltpu.VMEM((ROWS, TS + SUBLANES, QKV), F32),
        pltpu.VMEM((TM, BR), F32),
        pltpu.VMEM((TM, BR), F32),
        pltpu.VMEM((TM, BR), F32),
        pltpu.VMEM((TM, LANES), F32),
        pltpu.VMEM((TM, BR), F32),
        pltpu.VMEM((TM, BR), F32),
        pltpu.VMEM((TM, BR), F32),
        pltpu.VMEM((TM, BR), F32),
        pltpu.VMEM((TM, BR), F32),
        pltpu.VMEM((TM, BR), F32),
        pltpu.VMEM((TM, 2 * D_MODEL), F32),
        pltpu.VMEM((TM, BR), BF16),
        pltpu.VMEM((TM, BR), BF16),
        pltpu.VMEM((TM // GC * HEADS, GC, GC), BF16),
        pltpu.VMEM((ROWS * HEADS, HD, HD), F32),
        pltpu.VMEM((ROWS * HEADS, HD, HD), F32),
    ]
    return pl.pallas_call(
        functools.partial(_layer_kernel, layer, depth, is_last),
        grid=(bsz // ROWS, seq // TS),
        in_specs=[tile] + [_full(w.shape) for w in weights],
        out_specs=tile,
        out_shape=jax.ShapeDtypeStruct(h.shape, F32),
        scratch_shapes=scratch,
        compiler_params=pltpu.CompilerParams(
            dimension_semantics=("arbitrary", "arbitrary"), vmem_limit_bytes=VMEM_LIMIT),
        name=f"hybrid_layer{layer}",
    )(h, *weights)


def kernel(x, norm_w, w_in, conv_w, a_log, dt_bias, gdn_norm_w, hg_lb_logits, hg_norm_w, w_branch,
           w_out, final_norm_w):
    depth = w_in.shape[0]
    o_ab = QKV
    o_za = o_ab + 2 * HEADS
    o_hq = o_za + BR
    o_hf = o_hq + BR
    o_hi = o_hf + BR
    o_zb = o_hi + BR
    o_g = o_zb + BR

    def pad_lanes(t):
        return jnp.pad(t, ((0, 0), (0, LANES - t.shape[1])))

    def head_rows(t):
        return jnp.broadcast_to(jnp.pad(t, (0, SUBLANES - HEADS))[:, None], (SUBLANES, GC))

    lev = _level_matrix()
    h = x
    for l in range(depth):
        wl = w_in[l].astype(BF16)
        p = {
            "lev": lev,
            "nw": norm_w[l][None, :],
            "wqkv": wl[:, :o_ab],
            "wab": pad_lanes(wl[:, o_ab:o_za]),
            "wza": wl[:, o_za:o_hq],
            "whq": wl[:, o_hq:o_hf],
            "whf": wl[:, o_hf:o_hi],
            "whi": wl[:, o_hi:o_zb],
            "wzb": wl[:, o_zb:o_g],
            "wg": wl[:, o_g:],
            "cw": conv_w[l],
            "alog": head_rows(a_log[l]),
            "dtb": head_rows(dt_bias[l]),
            "gnw": gdn_norm_w[l][None, :],
            "lbl": hg_lb_logits,
            "hnw": hg_norm_w[l][None, :],
            "wa": w_branch[l, 0].astype(BF16),
            "wb": w_branch[l, 1].astype(BF16),
            "wo": w_out[l].astype(BF16),
            "fnw": final_norm_w[None, :],
        }
        h = _layer(h, l, depth, l == depth - 1, p)
    return h
```
